```python
import math
import jax
import jax.numpy as jnp
from jax import lax
import numpy as np

D_MODEL = 1024
BATCH = 16
SEQ = 4096
DEPTH = 2

GRID_W = 64
CTX_LEN = 256

BRANCH_W = 512
N_BRANCH = 3
LRU_BLOCKS = 8
LRU_BLOCK_W = BRANCH_W // LRU_BLOCKS
LRU_CONV_W = 4
LRU_PAD = (2, 1)
LRU_C = 8.0
HY_ORDER = 2
HY_PROJ = (HY_ORDER + 1) * BRANCH_W
HY_CONV_W = 3
HY_PAD = (1, 1)
HY_BANDS = 16
HY_EMB = 1 + 2 * HY_BANDS
HY_FFN = 64
HY_TARGET = 1e-2
HY_FAST_DECAY = 0.3
HY_SLOW_DECAY = 1.5
SSM_HEADDIM = 64
SSM_HEADS = BRANCH_W // SSM_HEADDIM
SSM_GROUPS = 2
SSM_STATE = 128
SSM_CONV_W = 4
SSM_PAD = (2, 1)
SSM_CHUNK = 128
SSM_BC = SSM_GROUPS * SSM_STATE
SSM_XBC = BRANCH_W + 2 * SSM_BC
FFN_HIDDEN = 2816
FFN_CONV_W = 3
RMS_EPS = 1e-6

STATE_COLS = BRANCH_W + SSM_XBC + 2 * SSM_HEADS
OUT_COLS = BRANCH_W + HY_PROJ + BRANCH_W + N_BRANCH * D_MODEL
IN_COLS = STATE_COLS + OUT_COLS

kernel_name = 'hybrid_lru_hyena_ssd_diffusion_block'


def _rmsnorm(u, g):
    u32 = u.astype(jnp.float32)
    y = u32 * lax.rsqrt(jnp.mean(u32 * u32, axis=-1, keepdims=True) + RMS_EPS)
    return (y * g.astype(jnp.float32)).astype(u.dtype)


def _dwconv1d(u, w, b, pad):
    out = lax.conv_general_dilated(u, w[:, None, :], window_strides=(1,), padding=(pad,),
                                   dimension_numbers=('NWC', 'WIO', 'NWC'),
                                   feature_group_count=u.shape[-1])
    return out + b


def _dwconv_grid(u, w, b, grid_h, grid_w):
    bsz, length, ch = u.shape
    img = u.reshape(bsz, grid_h, grid_w, ch)
    out = lax.conv_general_dilated(img, w[:, :, None, :], (1, 1), ((1, 1), (1, 1)),
                                   dimension_numbers=('NHWC', 'HWIO', 'NHWC'),
                                   feature_group_count=ch)
    return out.reshape(bsz, length, ch) + b


def _lin_combine(left, right):
    a_l, b_l = left
    a_r, b_r = right
    return a_l * a_r, a_r * b_l + b_r


def _linear_scan(a, b, h0):
    b = b.at[:, 0].add(a[:, 0] * h0)
    _, h = lax.associative_scan(_lin_combine, (a, b), axis=1)
    return h


def _rglru_dir(xc, p, d, h0):
    bsz, length, _ = xc.shape
    xb = xc.reshape(bsz, length, LRU_BLOCKS, LRU_BLOCK_W)
    r = jax.nn.sigmoid((jnp.einsum('blnk,nkj->blnj', xb, p['lru_w_a'][d]).reshape(bsz, length, BRANCH_W)
                        + p['lru_b_a'][d]).astype(jnp.float32))
    i = jax.nn.sigmoid((jnp.einsum('blnk,nkj->blnj', xb, p['lru_w_i'][d]).reshape(bsz, length, BRANCH_W)
                        + p['lru_b_i'][d]).astype(jnp.float32))
    log_a = -LRU_C * r * jax.nn.softplus(-p['lru_lambda'][d].astype(jnp.float32))
    a = jnp.exp(log_a)
    gated_x = jnp.sqrt(-jnp.expm1(2.0 * log_a)) * (i * xc.astype(jnp.float32))
    h = _linear_scan(a, gated_x, h0)
    return h, h[:, -1]


def _hyena_filters(length, p):
    pos = jnp.arange(length, dtype=jnp.float32)[:, None]
    t01 = jnp.linspace(0.0, 1.0, length, dtype=jnp.float32)[:, None]
    bands = jnp.linspace(1e-4, HY_BANDS - 1, HY_BANDS, dtype=jnp.float32)
    ang = (2.0 * math.pi / length) * pos * bands
    feats = jnp.concatenate([t01, jnp.cos(ang), jnp.sin(ang)], axis=-1)
    freq = p['hy_freq'].astype(jnp.float32)
    h = jnp.sin(freq[0] * (feats @ p['hy_w1'].astype(jnp.float32) + p['hy_b1'].astype(jnp.float32)))
    h = jnp.sin(freq[1] * (h @ p['hy_w2'].astype(jnp.float32) + p['hy_b2'].astype(jnp.float32)))
    h = (h @ p['hy_w3'].astype(jnp.float32)).reshape(length, HY_ORDER, 2, BRANCH_W)
    max_decay = math.log(HY_TARGET) / HY_FAST_DECAY
    min_decay = math.log(HY_TARGET) / HY_SLOW_DECAY
    deltas = jnp.abs(jnp.linspace(min_decay, max_decay, BRANCH_W, dtype=jnp.float32))
    window = jnp.exp(-t01 * deltas)
    return h * window[:, None, None, :]


def _bidir_long_conv(z, h_fwd, h_bwd, bias):
    length = z.shape[1]
    taps = jnp.concatenate([h_fwd, jnp.zeros_like(h_fwd[:1]), h_bwd[:0:-1]], axis=0)
    zf = jnp.fft.rfft(z.astype(jnp.float32), n=2 * length, axis=1)
    kf = jnp.fft.rfft(taps, n=2 * length, axis=0)
    y = jnp.fft.irfft(zf * kf[None], n=2 * length, axis=1)[:, :length]
    return (y + z.astype(jnp.float32) * bias.astype(jnp.float32)).astype(z.dtype)


def _hyena(u, p):
    uc = _dwconv1d(u, p['hy_conv_w'], p['hy_conv_b'], HY_PAD)
    v, x1, x2 = jnp.split(uc, 3, axis=-1)
    filt = _hyena_filters(u.shape[1], p)
    z = x1 * _bidir_long_conv(v, filt[:, 0, 0], filt[:, 0, 1], p['hy_bias'][0])
    return x2 * _bidir_long_conv(z, filt[:, 1, 0], filt[:, 1, 1], p['hy_bias'][1])


def _ssd(xs, dt, a_neg, bm, cm, h0, with_output):
    bsz, length = xs.shape[:2]
    nc, q, g, e = length // SSM_CHUNK, SSM_CHUNK, SSM_GROUPS, SSM_HEADS // SSM_GROUPS
    x = xs.reshape(bsz, nc, q, g, e, SSM_HEADDIM)
    dtc = dt.reshape(bsz, nc, q, g, e)
    xdt = x * dtc[..., None]
    bc = bm.reshape(bsz, nc, q, g, SSM_STATE)
    cc = cm.reshape(bsz, nc, q, g, SSM_STATE)
    cs = jnp.cumsum(dtc * a_neg.reshape(g, e), axis=2)
    decay_to_end = jnp.exp(cs[:, :, -1:] - cs)
    states = jnp.einsum('bcsgn,bcsge,bcsgep->bcgepn', bc, decay_to_end, xdt)
    chunk_decay = jnp.exp(cs[:, :, -1])[..., None, None]
    h0g = h0.reshape(bsz, g, e, SSM_HEADDIM, SSM_STATE)
    states = states.at[:, 0].add(chunk_decay[:, 0] * h0g)
    _, h_out = lax.associative_scan(_lin_combine, (chunk_decay, states), axis=1)
    final = h_out[:, -1].reshape(bsz, SSM_HEADS, SSM_HEADDIM, SSM_STATE)
    if not with_output:
        return None, final
    h_in = jnp.concatenate([h0g[:, None], h_out[:, :-1]], axis=1)
    seg = cs[:, :, :, None] - cs[:, :, None]
    lower = jnp.tril(jnp.ones((q, q), dtype=bool))[:, :, None, None]
    lmat = jnp.exp(jnp.where(lower, seg, -jnp.inf))
    cb = jnp.einsum('bcqgn,bcsgn->bcqsg', cc, bc)
    y_diag = jnp.einsum('bcqsg,bcqsge,bcsgep->bcqgep', cb, lmat, xdt)
    y_off = jnp.einsum('bcqgn,bcgepn,bcqge->bcqgep', cc, h_in, jnp.exp(cs))
    return (y_diag + y_off).reshape(bsz, length, SSM_HEADS, SSM_HEADDIM), final


def _token_mixer(hm, p, init, with_output):
    bsz, length, _ = hm.shape
    f32 = jnp.float32
    lru_f0, lru_b0, ssm_f0, ssm_b0 = init
    ps = hm @ p['w_in'][:, :STATE_COLS]
    xa = ps[..., :BRANCH_W]
    xbc = ps[..., BRANCH_W:BRANCH_W + SSM_XBC]
    dt_raw = ps[..., BRANCH_W + SSM_XBC:].reshape(bsz, length, 2, SSM_HEADS)

    xa_c = _dwconv1d(xa, p['lru_conv_w'], p['lru_conv_b'], LRU_PAD)
    ha_f, fin_a_f = _rglru_dir(xa_c, p, 0, lru_f0)
    ha_b, fin_a_b = _rglru_dir(jnp.flip(xa_c, 1), p, 1, lru_b0)

    xbc_c = jax.nn.silu(_dwconv1d(xbc, p['ssm_conv_w'], p['ssm_conv_b'], SSM_PAD)).astype(f32)
    xs = xbc_c[..., :BRANCH_W].reshape(bsz, length, SSM_HEADS, SSM_HEADDIM)
    bm = xbc_c[..., BRANCH_W:BRANCH_W + SSM_BC].reshape(bsz, length, SSM_GROUPS, SSM_STATE)
    cm = xbc_c[..., BRANCH_W + SSM_BC:].reshape(bsz, length, SSM_GROUPS, SSM_STATE)
    dt = jax.nn.softplus(dt_raw.astype(f32) + p['ssm_dt_bias'].astype(f32))
    a_neg = -jnp.exp(p['ssm_a_log'].astype(f32))
    ys_f, fin_s_f = _ssd(xs, dt[:, :, 0], a_neg[0], bm, cm, ssm_f0, with_output)
    ys_b, fin_s_b = _ssd(jnp.flip(xs, 1), jnp.flip(dt[:, :, 1], 1), a_neg[1],
                         jnp.flip(bm, 1), jnp.flip(cm, 1), ssm_b0, with_output)
    finals = (fin_a_f, fin_a_b, fin_s_f, fin_s_b)
    if not with_output:
        return None, finals

    pr = hm @ p['w_in'][:, STATE_COLS:]
    ga = pr[..., :BRANCH_W]
    hy_in = pr[..., BRANCH_W:BRANCH_W + HY_PROJ]
    z = pr[..., BRANCH_W + HY_PROJ:2 * BRANCH_W + HY_PROJ]
    gate_logits = pr[..., 2 * BRANCH_W + HY_PROJ:]

    y_a = (ha_f + jnp.flip(ha_b, 1)).astype(hm.dtype) * jax.nn.gelu(ga)
    y_b = _hyena(hy_in, p)
    ys = ys_f + jnp.flip(ys_b, 1) + p['ssm_d'].astype(f32)[:, None] * xs
    y_c = _rmsnorm(ys.reshape(bsz, length, BRANCH_W) * jax.nn.silu(z.astype(f32)), p['ssm_norm']).astype(hm.dtype)

    branch_outs = (y_a, y_b, y_c)
    merged = jnp.zeros((bsz, length, D_MODEL), f32)
    for k in range(N_BRANCH):
        gate = jax.nn.sigmoid(gate_logits[..., k * D_MODEL:(k + 1) * D_MODEL].astype(f32))
        merged = merged + gate * (branch_outs[k] @ p['w_branch'][k])
    return merged.astype(hm.dtype) @ p['w_out'], finals


def _conv_ffn(h, p, grid_h, grid_w):
    u = _dwconv_grid(h @ p['ffn_w_up'], p['ffn_conv_w'], p['ffn_conv_b'], grid_h, grid_w)
    val, gate = jnp.split(u, 2, axis=-1)
    return (jax.nn.gelu(gate) * val) @ p['ffn_w_down']


def setup_inputs(seed: int = 0) -> dict:
    key = jax.random.key(seed)
    ks = iter(jax.random.split(key, 48))
    f32 = jnp.float32

    def nrm(shape, scale):
        return scale * jax.random.normal(next(ks), shape, f32)

    def unif(shape, lo, hi):
        return jax.random.uniform(next(ks), shape, f32, lo, hi)

    x = nrm((BATCH, SEQ, D_MODEL), 1.0)
    c = nrm((BATCH, D_MODEL), 1.0)
    ctx = nrm((BATCH, CTX_LEN, D_MODEL), 1.0)
    c_ctx = nrm((D_MODEL,), 1.0)
    mod_w = nrm((DEPTH, D_MODEL, 6 * D_MODEL), 0.5 * D_MODEL ** -0.5)
    mod_b = nrm((DEPTH, 6 * D_MODEL), 0.02)
    norms = 1.0 + nrm((DEPTH, 4, D_MODEL), 0.05)
    w_in = nrm((DEPTH, D_MODEL, IN_COLS), D_MODEL ** -0.5)
    lru_conv_w = nrm((DEPTH, LRU_CONV_W, BRANCH_W), LRU_CONV_W ** -0.5)
    lru_conv_b = nrm((DEPTH, BRANCH_W), 0.02)
    lru_w_a = nrm((DEPTH, 2, LRU_BLOCKS, LRU_BLOCK_W, LRU_BLOCK_W), LRU_BLOCK_W ** -0.5)
    lru_b_a = nrm((DEPTH, 2, BRANCH_W), 0.02)
    lru_w_i = nrm((DEPTH, 2, LRU_BLOCKS, LRU_BLOCK_W, LRU_BLOCK_W), LRU_BLOCK_W ** -0.5)
    lru_b_i = nrm((DEPTH, 2, BRANCH_W), 0.02)
    a_pow = unif((DEPTH, 2, BRANCH_W), 0.9, 0.999)
    a_base = a_pow ** (1.0 / LRU_C)
    lru_lambda = jnp.log(a_base) - jnp.log1p(-a_base)
    hy_conv_w = nrm((DEPTH, HY_CONV_W, HY_PROJ), HY_CONV_W ** -0.5)
    hy_conv_b = nrm((DEPTH, HY_PROJ), 0.02)
    hy_w1 = nrm((DEPTH, HY_EMB, HY_FFN), HY_EMB ** -0.5)
    hy_b1 = nrm((DEPTH, HY_FFN), 0.02)
    hy_w2 = nrm((DEPTH, HY_FFN, HY_FFN), HY_FFN ** -0.5)
    hy_b2 = nrm((DEPTH, HY_FFN), 0.02)
    hy_w3 = nrm((DEPTH, HY_FFN, HY_ORDER * 2 * BRANCH_W), 0.1 * HY_FFN ** -0.5)
    hy_freq = 1.0 + nrm((DEPTH, 2, HY_FFN), 0.05)
    hy_bias = nrm((DEPTH, HY_ORDER, BRANCH_W), 0.5)
    ssm_conv_w = nrm((DEPTH, SSM_CONV_W, SSM_XBC), SSM_CONV_W ** -0.5)
    ssm_conv_b = nrm((DEPTH, SSM_XBC), 0.02)
    dt0 = jnp.exp(unif((DEPTH, 2, SSM_HEADS), math.log(1e-3), math.log(1e-1)))
    ssm_dt_bias = dt0 + jnp.log(-jnp.expm1(-dt0))
    ssm_a_log = jnp.log(unif((DEPTH, 2, SSM_HEADS), 1.0, 16.0))
    ssm_d = 1.0 + nrm((DEPTH, SSM_HEADS), 0.05)
    ssm_norm = 1.0 + nrm((DEPTH, BRANCH_W), 0.05)
    w_branch = nrm((DEPTH, N_BRANCH, BRANCH_W, D_MODEL), BRANCH_W ** -0.5)
    w_out = nrm((DEPTH, D_MODEL, D_MODEL), D_MODEL ** -0.5)
    ffn_w_up = nrm((DEPTH, D_MODEL, 2 * FFN_HIDDEN), D_MODEL ** -0.5)
    ffn_conv_w = nrm((DEPTH, FFN_CONV_W, FFN_CONV_W, 2 * FFN_HIDDEN), 1.0 / FFN_CONV_W)
    ffn_conv_b = nrm((DEPTH, 2 * FFN_HIDDEN), 0.02)
    ffn_w_down = nrm((DEPTH, FFN_HIDDEN, D_MODEL), FFN_HIDDEN ** -0.5)
    return {'x': x, 'c': c, 'ctx': ctx, 'c_ctx': c_ctx, 'mod_w': mod_w, 'mod_b': mod_b,
            'norms': norms, 'w_in': w_in, 'lru_conv_w': lru_conv_w, 'lru_conv_b': lru_conv_b,
            'lru_w_a': lru_w_a, 'lru_b_a': lru_b_a, 'lru_w_i': lru_w_i, 'lru_b_i': lru_b_i,
            'lru_lambda': lru_lambda, 'hy_conv_w': hy_conv_w, 'hy_conv_b': hy_conv_b,
            'hy_w1': hy_w1, 'hy_b1': hy_b1, 'hy_w2': hy_w2, 'hy_b2': hy_b2, 'hy_w3': hy_w3,
            'hy_freq': hy_freq, 'hy_bias': hy_bias, 'ssm_conv_w': ssm_conv_w,
            'ssm_conv_b': ssm_conv_b, 'ssm_dt_bias': ssm_dt_bias, 'ssm_a_log': ssm_a_log,
            'ssm_d': ssm_d, 'ssm_norm': ssm_norm, 'w_branch': w_branch, 'w_out': w_out,
            'ffn_w_up': ffn_w_up, 'ffn_conv_w': ffn_conv_w, 'ffn_conv_b': ffn_conv_b,
            'ffn_w_down': ffn_w_down}


def reference(x, c, ctx, c_ctx, mod_w, mod_b, norms, w_in, lru_conv_w, lru_conv_b,
              lru_w_a, lru_b_a, lru_w_i, lru_b_i, lru_lambda, hy_conv_w, hy_conv_b,
              hy_w1, hy_b1, hy_w2, hy_b2, hy_w3, hy_freq, hy_bias, ssm_conv_w, ssm_conv_b,
              ssm_dt_bias, ssm_a_log, ssm_d, ssm_norm, w_branch, w_out, ffn_w_up,
              ffn_conv_w, ffn_conv_b, ffn_w_down):
    bsz = x.shape[0]
    rows = x.shape[1] // GRID_W
    ctx_len = ctx.shape[1]
    f32 = jnp.float32
    zero_init = (jnp.zeros((bsz, BRANCH_W), f32), jnp.zeros((bsz, BRANCH_W), f32),
                 jnp.zeros((bsz, SSM_HEADS, SSM_HEADDIM, SSM_STATE), f32),
                 jnp.zeros((bsz, SSM_HEADS, SSM_HEADDIM, SSM_STATE), f32))
    for l in range(DEPTH):
        last = l == DEPTH - 1
        p = {'w_in': w_in[l], 'lru_conv_w': lru_conv_w[l], 'lru_conv_b': lru_conv_b[l],
             'lru_w_a': lru_w_a[l], 'lru_b_a': lru_b_a[l], 'lru_w_i': lru_w_i[l],
             'lru_b_i': lru_b_i[l], 'lru_lambda': lru_lambda[l], 'hy_conv_w': hy_conv_w[l],
             'hy_conv_b': hy_conv_b[l], 'hy_w1': hy_w1[l], 'hy_b1': hy_b1[l], 'hy_w2': hy_w2[l],
             'hy_b2': hy_b2[l], 'hy_w3': hy_w3[l], 'hy_freq': hy_freq[l], 'hy_bias': hy_bias[l],
             'ssm_conv_w': ssm_conv_w[l], 'ssm_conv_b': ssm_conv_b[l],
             'ssm_dt_bias': ssm_dt_bias[l], 'ssm_a_log': ssm_a_log[l], 'ssm_d': ssm_d[l],
             'ssm_norm': ssm_norm[l], 'w_branch': w_branch[l], 'w_out': w_out[l],
             'ffn_w_up': ffn_w_up[l], 'ffn_conv_w': ffn_conv_w[l], 'ffn_conv_b': ffn_conv_b[l],
             'ffn_w_down': ffn_w_down[l]}
        mod = jax.nn.silu(c) @ mod_w[l] + mod_b[l]
        sh1, sc1, g1, sh2, sc2, g2 = jnp.split(mod[:, None, :], 6, axis=-1)
        mod_c = jax.nn.silu(c_ctx) @ mod_w[l] + mod_b[l]
        csh1, csc1, cg1, csh2, csc2, cg2 = jnp.split(mod_c, 6)

        hc = _rmsnorm(ctx, norms[l, 0]) * (1.0 + csc1) + csh1
        ctx_out, ctx_states = _token_mixer(hc, p, zero_init, not last)
        hx = _rmsnorm(x, norms[l, 0]) * (1.0 + sc1) + sh1
        x_out, _ = _token_mixer(hx, p, ctx_states, True)
        x = x + g1 * _rmsnorm(x_out, norms[l, 1])

        hx2 = _rmsnorm(x, norms[l, 2]) * (1.0 + sc2) + sh2
        x = x + g2 * _rmsnorm(_conv_ffn(hx2, p, rows, GRID_W), norms[l, 3])

        if not last:
            ctx = ctx + cg1 * _rmsnorm(ctx_out, norms[l, 1])
            hc2 = _rmsnorm(ctx, norms[l, 2]) * (1.0 + csc2) + csh2
            ctx = ctx + cg2 * _rmsnorm(_conv_ffn(hc2, p, 1, ctx_len), norms[l, 3])
    return x
```

```python
import functools
import math

import jax
import jax.numpy as jnp
from jax import lax
from jax.experimental import pallas as pl
from jax.experimental.pallas import tpu as pltpu

F32 = jnp.float32
BF16 = jnp.bfloat16
HIGHEST = lax.Precision.HIGHEST

D_MODEL = 1024
DEPTH = 2
GRID_W = 64
BRANCH_W = 512
N_BRANCH = 3
LRU_BLOCKS = 8
LRU_BLOCK_W = BRANCH_W // LRU_BLOCKS
LRU_C = 8.0
HY_ORDER = 2
HY_PROJ = (HY_ORDER + 1) * BRANCH_W
HY_BANDS = 16
HY_EMB = 1 + 2 * HY_BANDS
HY_FFN = 64
HY_TARGET = 1e-2
HY_FAST_DECAY = 0.3
HY_SLOW_DECAY = 1.5
SSM_HEADDIM = 64
SSM_HEADS = BRANCH_W // SSM_HEADDIM
SSM_GROUPS = 2
SSM_STATE = 128
SSM_CHUNK = 128
SSM_BC = SSM_GROUPS * SSM_STATE
SSM_XBC = BRANCH_W + 2 * SSM_BC
FFN_HIDDEN = 2816
RMS_EPS = 1e-6
STATE_COLS = BRANCH_W + SSM_XBC + 2 * SSM_HEADS

COL_XA = 0
COL_XS = 512
COL_BC = 1024
COL_GA = 1536
COL_HY = 2048
COL_Z = 3584
COL_GATES = 4096
MAIN_COLS = 7168
HEADS_PER_GROUP = SSM_HEADS // SSM_GROUPS
GROUP_W = BRANCH_W // SSM_GROUPS
LANES = 128
FFT_N2 = 128
FFT_MIN_LEN = 1024
VMEM_BIG = 56 * 1024 * 1024


def _cp(sem, vmem=None):
    return pltpu.CompilerParams(dimension_semantics=sem, vmem_limit_bytes=vmem)


def _rms(u, g):
    return u * lax.rsqrt(jnp.mean(u * u, axis=-1, keepdims=True) + RMS_EPS) * g


def _softplus(x):
    return jnp.maximum(x, 0.0) + jnp.log1p(jnp.exp(-jnp.abs(x)))


def _mod_kernel(c_ref, w_ref, b_ref, o_ref):
    c = c_ref[...]
    o_ref[0] = jnp.dot(jax.nn.silu(c), w_ref[0], precision=HIGHEST,
                       preferred_element_type=F32) + b_ref[0]


def _modulation(cc, mod_w, mod_b):
    rows = cc.shape[0]
    depth, _, ncol = mod_w.shape
    tn = 1024
    return pl.pallas_call(
        _mod_kernel,
        grid=(depth, ncol // tn),
        in_specs=[pl.BlockSpec((rows, D_MODEL), lambda l, j: (0, 0)),
                  pl.BlockSpec((1, D_MODEL, tn), lambda l, j: (l, 0, j)),
                  pl.BlockSpec((1, 1, tn), lambda l, j: (l, 0, j))],
        out_specs=pl.BlockSpec((1, rows, tn), lambda l, j: (l, 0, j)),
        out_shape=jax.ShapeDtypeStruct((depth, rows, ncol), F32),
        compiler_params=_cp(("parallel", "parallel")),
        name="modulation",
    )(cc, mod_w, mod_b.reshape(depth, 1, ncol))


def _norm_mm_kernel(x_ref, g_ref, sh_ref, sc_ref, w_ref, *rest, has_dt):
    if has_dt:
        wdt_ref, o_ref, odt_ref, hx_ref = rest
    else:
        o_ref, hx_ref = rest

    @pl.when(pl.program_id(2) == 0)
    def _():
        h = _rms(x_ref[0], g_ref[...]) * (1.0 + sc_ref[0]) + sh_ref[0]
        hx_ref[...] = h.astype(BF16)
        if has_dt:
            odt_ref[0] = jnp.dot(hx_ref[...], wdt_ref[...], preferred_element_type=F32)

    o_ref[0] = jnp.dot(hx_ref[...], w_ref[...], preferred_element_type=F32).astype(o_ref.dtype)


def _norm_mm(x, gnorm, mod, row_of, k_shift, k_scale, w, w_dt=None, tn=1024):
    bsz, length, _ = x.shape
    ncol = w.shape[1]
    tm = min(1024, length)
    has_dt = w_dt is not None
    in_specs = [pl.BlockSpec((1, tm, D_MODEL), lambda b, i, j: (b, i, 0)),
                pl.BlockSpec((1, D_MODEL), lambda b, i, j: (0, 0)),
                pl.BlockSpec((1, 1, D_MODEL), lambda b, i, j: (row_of(b), 0, k_shift)),
                pl.BlockSpec((1, 1, D_MODEL), lambda b, i, j: (row_of(b), 0, k_scale)),
                pl.BlockSpec((D_MODEL, tn), lambda b, i, j: (0, j))]
    args = [x, gnorm.reshape(1, D_MODEL), mod, mod, w]
    out_specs = [pl.BlockSpec((1, tm, tn), lambda b, i, j: (b, i, j))]
    out_shape = [jax.ShapeDtypeStruct((bsz, length, ncol), BF16)]
    if has_dt:
        ndt = w_dt.shape[1]
        in_specs.append(pl.BlockSpec((D_MODEL, ndt), lambda b, i, j: (0, 0)))
        args.append(w_dt)
        out_specs.append(pl.BlockSpec((1, tm, ndt), lambda b, i, j: (b, i, 0)))
        out_shape.append(jax.ShapeDtypeStruct((bsz, length, ndt), F32))
    res = pl.pallas_call(
        functools.partial(_norm_mm_kernel, has_dt=has_dt),
        grid=(bsz, length // tm, ncol // tn),
        in_specs=in_specs, out_specs=out_specs, out_shape=out_shape,
        scratch_shapes=[pltpu.VMEM((tm, D_MODEL), BF16)],
        compiler_params=_cp(("parallel", "parallel", "arbitrary"), VMEM_BIG),
        name="norm_proj",
    )(*args)
    return res if has_dt else res[0]


PAD = 8


def _fill_padded(pad_ref, src_ref, length, width, chunk):
    zeros = jnp.zeros((PAD, width), F32)
    pad_ref[pl.ds(0, PAD), :] = zeros
    pad_ref[pl.ds(PAD + length, PAD), :] = zeros

    def body(i, carry):
        t0 = pl.multiple_of(i * chunk, chunk)
        pad_ref[pl.ds(PAD + t0, chunk), :] = src_ref[0, pl.ds(t0, chunk), :].astype(F32)
        return carry

    lax.fori_loop(0, length // chunk, body, 0)


def _conv_rows(pad_ref, t0, rows, w, b, left):
    win = pad_ref[pl.ds(t0, rows + 2 * PAD), :]
    acc = b
    for k in range(w.shape[0]):
        off = PAD + k - left
        acc = acc + w[k:k + 1, :] * win[off:off + rows, :]
    return acc


LRU_CHUNK = 128


def _scan_steps(rows):
    s, out = 1, []
    while s < rows:
        out.append(s)
        s *= 2
    return out


def _chunk_scan_fwd(a, b):
    rows = a.shape[0]
    ridx = lax.broadcasted_iota(jnp.int32, a.shape, 0)
    for s in _scan_steps(rows):
        m = ridx >= s
        a_s = jnp.where(m, pltpu.roll(a, s, 0), 1.0)
        b_s = jnp.where(m, pltpu.roll(b, s, 0), 0.0)
        b = a * b_s + b
        a = a * a_s
    return a, b


def _chunk_scan_bwd(a, b):
    rows = a.shape[0]
    ridx = lax.broadcasted_iota(jnp.int32, a.shape, 0)
    for s in _scan_steps(rows):
        m = ridx < rows - s
        a_s = jnp.where(m, pltpu.roll(a, rows - s, 0), 1.0)
        b_s = jnp.where(m, pltpu.roll(b, rows - s, 0), 0.0)
        b = a * b_s + b
        a = a * a_s
    return a, b


def _lru_kernel(xa_ref, ga_ref, cw_ref, cb_ref, waf_ref, wif_ref, wab_ref, wib_ref,
                bias_ref, lam_ref, h0_ref, y_ref, fin_ref, pad_ref, hf_ref, *, length):
    width = xa_ref.shape[-1]
    tc = LRU_CHUNK
    nchunk = length // tc
    _fill_padded(pad_ref, xa_ref, length, width, tc)
    cw = cw_ref[...]
    cb = cb_ref[...]
    neg_c_sp = -LRU_C * _softplus(-lam_ref[...])

    def gated(t0, wa_ref, wi_ref, d):
        xc = _conv_rows(pad_ref, t0, tc, cw, cb, 2)
        xcb = xc.astype(BF16)
        r = jax.nn.sigmoid(jnp.dot(xcb, wa_ref[...], preferred_element_type=F32)
                           + bias_ref[2 * d:2 * d + 1, :])
        i = jax.nn.sigmoid(jnp.dot(xcb, wi_ref[...], preferred_element_type=F32)
                           + bias_ref[2 * d + 1:2 * d + 2, :])
        log_a = neg_c_sp[d:d + 1, :] * r
        a = jnp.exp(log_a)
        gx = jnp.sqrt(jnp.tanh(-log_a) * (1.0 + a * a)) * (i * xc)
        return a, gx

    def fwd_body(c, h):
        t0 = pl.multiple_of(c * tc, tc)
        a, gx = gated(t0, waf_ref, wif_ref, 0)
        a_cum, b_cum = _chunk_scan_fwd(a, gx)
        hs = a_cum * h + b_cum
        hf_ref[pl.ds(t0, tc), :] = hs
        return hs[tc - 1:tc, :]

    h_f = lax.fori_loop(0, nchunk, fwd_body, h0_ref[0, 0:1, :])
    fin_ref[0, 0:1, :] = h_f

    def bwd_body(c, h):
        t0 = pl.multiple_of((nchunk - 1 - c) * tc, tc)
        a, gx = gated(t0, wab_ref, wib_ref, 1)
        a_cum, b_cum = _chunk_scan_bwd(a, gx)
        hs = a_cum * h + b_cum
        ga = ga_ref[0, pl.ds(t0, tc), :].astype(F32)
        y = (hf_ref[pl.ds(t0, tc), :] + hs) * jax.nn.gelu(ga)
        y_ref[0, pl.ds(t0, tc), :] = y.astype(y_ref.dtype)
        return hs[0:1, :]

    h_b = lax.fori_loop(0, nchunk, bwd_body, h0_ref[0, 1:2, :])
    fin_ref[0, 1:2, :] = h_b


def _lru(proj, p, h0):
    bsz, length, _ = proj.shape
    cw_ = GROUP_W
    nsplit = BRANCH_W // cw_
    blk = lambda off: pl.BlockSpec((1, length, cw_), lambda b, j: (b, 0, off // cw_ + j))
    vec = lambda rows: pl.BlockSpec((rows, cw_), lambda b, j: (0, j))
    wspec = pl.BlockSpec((cw_, cw_), lambda b, j: (j, j))
    st = pl.BlockSpec((1, 2, cw_), lambda b, j: (b, 0, j))
    return pl.pallas_call(
        functools.partial(_lru_kernel, length=length),
        grid=(bsz, nsplit),
        in_specs=[blk(COL_XA), blk(COL_GA), vec(4), vec(1), wspec, wspec, wspec, wspec,
                  vec(4), vec(2), st],
        out_specs=[pl.BlockSpec((1, length, cw_), lambda b, j: (b, 0, j)), st],
        out_shape=[jax.ShapeDtypeStruct((bsz, length, BRANCH_W), BF16),
                   jax.ShapeDtypeStruct((bsz, 2, BRANCH_W), F32)],
        scratch_shapes=[pltpu.VMEM((length + 2 * PAD, cw_), F32),
                        pltpu.VMEM((length, cw_), F32)],
        compiler_params=_cp(("parallel", "parallel"), VMEM_BIG),
        name="rglru",
    )(proj, proj, p["lru_conv_w"], p["lru_conv_b"], p["lru_waf"], p["lru_wif"],
      p["lru_wab"], p["lru_wib"], p["lru_bias"], p["lru_lambda"], h0)


def _ssd_kernel(xs_ref, bc_ref, z_ref, dt_ref, cwx_ref, cbx_ref, cwb_ref, cbb_ref,
                dtb_ref, alog_ref, dch_ref, h0_ref, y_ref, fin_ref,
                padx_ref, padb_ref, xc_ref, yf_ref, st_ref, *, length):
    q = SSM_CHUNK
    nchunk = length // q
    gw = xs_ref.shape[-1]
    nh = HEADS_PER_GROUP
    hp = SSM_HEADDIM
    ns = SSM_STATE
    _fill_padded(padx_ref, xs_ref, length, gw, q)
    _fill_padded(padb_ref, bc_ref, length, 2 * ns, q)
    cwx, cbx, cwb, cbb = cwx_ref[...], cbx_ref[...], cwb_ref[...], cbb_ref[...]

    def conv_body(c, carry):
        t0 = pl.multiple_of(c * q, q)
        xc_ref[pl.ds(t0, q), 0:gw] = jax.nn.silu(_conv_rows(padx_ref, t0, q, cwx, cbx, 2)).astype(BF16)
        xc_ref[pl.ds(t0, q), gw:gw + 2 * ns] = jax.nn.silu(
            _conv_rows(padb_ref, t0, q, cwb, cbb, 2)).astype(BF16)
        return carry

    lax.fori_loop(0, nchunk, conv_body, 0)

    dtb = dtb_ref[...]
    a_neg = -jnp.exp(alog_ref[...])
    rid = lax.broadcasted_iota(jnp.int32, (q, q), 0)
    cid = lax.broadcasted_iota(jnp.int32, (q, q), 1)
    lower = rid >= cid
    upper = rid <= cid
    tri_l = lower.astype(F32)
    tri_u = upper.astype(F32)

    def chunk(t0, d):
        tri, keep, edge = (tri_l, lower, q - 1) if d == 0 else (tri_u, upper, 0)
        dt = _softplus(dt_ref[0, pl.ds(t0, q), :] + dtb)
        cs = jnp.dot(tri, dt * a_neg, precision=HIGHEST, preferred_element_type=F32)
        cs_t = cs.T
        xs = xc_ref[pl.ds(t0, q), 0:gw].astype(F32)
        bm = xc_ref[pl.ds(t0, q), gw:gw + ns]
        cm = xc_ref[pl.ds(t0, q), gw + ns:gw + 2 * ns]
        bm_t = bm.astype(F32).T.astype(BF16)
        cb = jnp.dot(cm, bm_t, preferred_element_type=F32)
        outs = []
        for j in range(nh):
            lane = d * nh + j
            cs_col = cs[:, lane:lane + 1]
            cs_row = cs_t[lane:lane + 1, :]
            lmat = jnp.exp(jnp.where(keep, cs_col - cs_row, -jnp.inf))
            xdt = xs[:, j * hp:(j + 1) * hp] * dt[:, lane:lane + 1]
            y_diag = jnp.dot((cb * lmat).astype(BF16), xdt.astype(BF16),
                             preferred_element_type=F32)
            st = st_ref[j]
            y_off = jnp.dot(cm, st.astype(BF16), preferred_element_type=F32) * jnp.exp(cs_col)
            outs.append(y_diag + y_off)
            cs_edge = cs[edge:edge + 1, lane:lane + 1]
            xw = (xdt * jnp.exp(cs_edge - cs_col)).astype(BF16)
            st_ref[j] = jnp.exp(cs_edge) * st + jnp.dot(bm_t, xw, preferred_element_type=F32)
        return jnp.concatenate(outs, axis=1), xs

    st_ref[...] = h0_ref[0, 0]

    def fwd_body(c, carry):
        t0 = pl.multiple_of(c * q, q)
        y, _ = chunk(t0, 0)
        yf_ref[pl.ds(t0, q), :] = y
        return carry

    lax.fori_loop(0, nchunk, fwd_body, 0)
    fin_ref[0, 0] = st_ref[...]
    st_ref[...] = h0_ref[0, 1]
    dch = dch_ref[...]

    def bwd_body(c, carry):
        t0 = pl.multiple_of((nchunk - 1 - c) * q, q)
        y, xs = chunk(t0, 1)
        ys = yf_ref[pl.ds(t0, q), :] + y + dch * xs
        z = z_ref[0, pl.ds(t0, q), :].astype(F32)
        y_ref[0, pl.ds(t0, q), :] = (ys * jax.nn.silu(z)).astype(y_ref.dtype)
        return carry

    lax.fori_loop(0, nchunk, bwd_body, 0)
    fin_ref[0, 1] = st_ref[...]


def _ssd(proj, dt, p, h0):
    bsz, length, _ = proj.shape
    gw, ns, nh, hp = GROUP_W, SSM_STATE, HEADS_PER_GROUP, SSM_HEADDIM
    blk = lambda off: pl.BlockSpec((1, length, gw), lambda b, g: (b, 0, off // gw + g))
    vec = lambda rows, w: pl.BlockSpec((rows, w), lambda b, g: (0, g))
    st = pl.BlockSpec((1, 2, nh, ns, hp), lambda b, g: (b, 0, g, 0, 0))
    return pl.pallas_call(
        functools.partial(_ssd_kernel, length=length),
        grid=(bsz, SSM_GROUPS),
        in_specs=[blk(COL_XS), blk(COL_BC), blk(COL_Z),
                  pl.BlockSpec((1, length, LANES), lambda b, g: (b, 0, g)),
                  vec(4, gw), vec(1, gw), vec(4, 2 * ns), vec(1, 2 * ns),
                  vec(1, LANES), vec(1, LANES), vec(1, gw), st],
        out_specs=[pl.BlockSpec((1, length, gw), lambda b, g: (b, 0, g)), st],
        out_shape=[jax.ShapeDtypeStruct((bsz, length, BRANCH_W), BF16),
                   jax.ShapeDtypeStruct((bsz, 2, SSM_HEADS, ns, hp), F32)],
        scratch_shapes=[pltpu.VMEM((length + 2 * PAD, gw), F32),
                        pltpu.VMEM((length + 2 * PAD, 2 * ns), F32),
                        pltpu.VMEM((length, gw + 2 * ns), BF16),
                        pltpu.VMEM((length, gw), F32),
                        pltpu.VMEM((nh, ns, hp), F32)],
        compiler_params=_cp(("parallel", "parallel"), VMEM_BIG),
        name="ssd",
    )(proj, proj, proj, dt, p["ssm_cw_x"], p["ssm_cb_x"], p["ssm_cw_bc"], p["ssm_cb_bc"],
      p["ssm_dtb"], p["ssm_alog"], p["ssm_dch"], h0)


HY_CONV_CHUNK = 256


def _hy_conv_kernel(u_ref, w_ref, b_ref, o_ref, pad_ref, *, length):
    width = u_ref.shape[-1]
    tc = min(HY_CONV_CHUNK, length)
    _fill_padded(pad_ref, u_ref, length, width, tc)
    w, b = w_ref[...], b_ref[...]

    def body(c, carry):
        t0 = pl.multiple_of(c * tc, tc)
        o_ref[0, 0, pl.ds(t0, tc), :] = _conv_rows(pad_ref, t0, tc, w, b, 1).astype(o_ref.dtype)
        return carry

    lax.fori_loop(0, length // tc, body, 0)


def _hy_conv(proj, p):
    bsz, length, _ = proj.shape
    w = BRANCH_W
    return pl.pallas_call(
        functools.partial(_hy_conv_kernel, length=length),
        grid=(bsz, 3),
        in_specs=[pl.BlockSpec((1, length, w), lambda b, j: (b, 0, COL_HY // w + j)),
                  pl.BlockSpec((3, w), lambda b, j: (0, j)),
                  pl.BlockSpec((1, w), lambda b, j: (0, j))],
        out_specs=pl.BlockSpec((1, 1, length, w), lambda b, j: (j, b, 0, 0)),
        out_shape=jax.ShapeDtypeStruct((3, bsz, length, w), BF16),
        scratch_shapes=[pltpu.VMEM((length + 2 * PAD, w), F32)],
        compiler_params=_cp(("parallel", "parallel"), VMEM_BIG),
        name="hyena_short_conv",
    )(proj, p["hy_conv_w"], p["hy_conv_b"])


def _hy_filter_kernel(f_ref, w1_ref, b1_ref, w2_ref, b2_ref, w3_ref, fr_ref, dl_ref, o_ref):
    feats = f_ref[...]
    h = jnp.sin(fr_ref[0:1, :] * (jnp.dot(feats, w1_ref[...], precision=HIGHEST,
                                          preferred_element_type=F32) + b1_ref[...]))
    h = jnp.sin(fr_ref[1:2, :] * (jnp.dot(h, w2_ref[...], precision=HIGHEST,
                                          preferred_element_type=F32) + b2_ref[...]))
    h = jnp.dot(h, w3_ref[...], precision=HIGHEST, preferred_element_type=F32)
    window = jnp.exp(-feats[:, 0:1] * dl_ref[...])
    o_ref[...] = h * jnp.concatenate([window] * (2 * HY_ORDER), axis=1)


def _hy_filters(length, p):
    pos = jnp.arange(length, dtype=F32)[:, None]
    t01 = jnp.linspace(0.0, 1.0, length, dtype=F32)[:, None]
    bands = jnp.linspace(1e-4, HY_BANDS - 1, HY_BANDS, dtype=F32)
    ang = (2.0 * math.pi / length) * pos * bands
    feats = jnp.concatenate([t01, jnp.cos(ang), jnp.sin(ang)], axis=-1)
    feats = jnp.pad(feats, ((0, 0), (0, LANES - HY_EMB)))
    max_decay = math.log(HY_TARGET) / HY_FAST_DECAY
    min_decay = math.log(HY_TARGET) / HY_SLOW_DECAY
    deltas = jnp.abs(jnp.linspace(min_decay, max_decay, BRANCH_W, dtype=F32))[None, :]
    tl = min(512, length)
    ncol = HY_ORDER * 2 * BRANCH_W
    full = lambda r, c: pl.BlockSpec((r, c), lambda i: (0, 0))
    return pl.pallas_call(
        _hy_filter_kernel,
        grid=(length // tl,),
        in_specs=[pl.BlockSpec((tl, LANES), lambda i: (i, 0)),
                  full(LANES, HY_FFN), full(1, HY_FFN), full(HY_FFN, HY_FFN), full(1, HY_FFN),
                  full(HY_FFN, ncol), full(2, HY_FFN), full(1, BRANCH_W)],
        out_specs=pl.BlockSpec((tl, ncol), lambda i: (i, 0)),
        out_shape=jax.ShapeDtypeStruct((length, ncol), F32),
        compiler_params=_cp(("parallel",)),
        name="hyena_filters",
    )(feats, p["hy_w1"], p["hy_b1"], p["hy_w2"], p["hy_b2"], p["hy_w3"], p["hy_freq"], deltas)


def _dft_tables(n1):
    n2 = FFT_N2
    n = n1 * n2
    two_pi = 2.0 * math.pi

    def cs(idx, period):
        ang = (two_pi / period) * (idx % period).astype(F32)
        return jnp.cos(ang), jnp.sin(ang)

    k1 = jnp.arange(n1, dtype=jnp.int32)
    c, s = cs(k1[:, None] * k1[None, :], n1)
    half = n1 // 2
    ch, sh = c[:, :half], s[:, :half]
    m1_re = jnp.concatenate([ch, sh], axis=1)
    m1_im = jnp.concatenate([-sh, ch], axis=1)
    m1 = jnp.stack([m1_re, m1_im], axis=1).reshape(2 * n1, n1)
    m1_taps = jnp.stack([c, -s], axis=1).reshape(2 * n1, n1)
    ci, si = ch.T / n, sh.T / n
    m2_re = jnp.stack([ci, -si], axis=2).reshape(half, 2 * n1)
    m2_im = jnp.stack([si, ci], axis=2).reshape(half, 2 * n1)
    m2 = jnp.concatenate([m2_re, m2_im], axis=0)
    k2 = jnp.arange(n2, dtype=jnp.int32)
    idx = k2[None, None, :] * (k2[None, :, None] * n1 + k1[:, None, None])
    gc, gs = cs(idx, n)
    g_fwd = jnp.concatenate([jnp.concatenate([gc, gs], axis=2),
                             jnp.concatenate([-gs, gc], axis=2)], axis=1)
    gct, gst = jnp.swapaxes(gc, 1, 2), jnp.swapaxes(gs, 1, 2)
    g_inv = jnp.concatenate([jnp.concatenate([gct, -gst], axis=2),
                             jnp.concatenate([gst, gct], axis=2)], axis=1)
    return m1, m1_taps, m2, g_fwd, g_inv


def _stage1_kernel(m_ref, *refs, nin, prec):
    o_ref = refs[-1]
    parts = [r[0] for r in refs[:nin]]
    x = parts[0] if nin == 1 else jnp.concatenate(parts, axis=0)
    o_ref[0] = jnp.dot(m_ref[...], x, precision=prec,
                       preferred_element_type=F32).astype(o_ref.dtype)


def _fft_chunk(ncols):
    return min(8192, ncols)


def _stage1_data(xv, m1):
    npair, n1, ncols = xv.shape
    ch = _fft_chunk(ncols)
    return pl.pallas_call(
        functools.partial(_stage1_kernel, nin=1, prec=None),
        grid=(npair, ncols // ch),
        in_specs=[pl.BlockSpec(m1.shape, lambda p, j: (0, 0)),
                  pl.BlockSpec((1, n1, ch), lambda p, j: (p, 0, j))],
        out_specs=pl.BlockSpec((1, 2 * n1, ch), lambda p, j: (p, 0, j)),
        out_shape=jax.ShapeDtypeStruct((npair, 2 * n1, ncols), BF16),
        compiler_params=_cp(("parallel", "parallel")),
        name="dft_stage1",
    )(m1, xv)


def _stage1_taps(taps, m1t):
    nord, n1, ncols = taps.shape
    ch = _fft_chunk(ncols)
    return pl.pallas_call(
        functools.partial(_stage1_kernel, nin=1, prec=HIGHEST),
        grid=(nord, ncols // ch),
        in_specs=[pl.BlockSpec(m1t.shape, lambda p, j: (0, 0)),
                  pl.BlockSpec((1, n1, ch), lambda p, j: (p, 0, j))],
        out_specs=pl.BlockSpec((1, 2 * n1, ch), lambda p, j: (p, 0, j)),
        out_shape=jax.ShapeDtypeStruct((nord, 2 * n1, ncols), F32),
        compiler_params=_cp(("parallel", "parallel")),
        name="dft_stage1_taps",
    )(m1t, taps)


def _mid_taps_kernel(g_ref, t_ref, o_ref):
    o_ref[0, 0] = jnp.dot(g_ref[0], t_ref[0, 0], precision=HIGHEST, preferred_element_type=F32)


def _mid_taps(t, g_fwd):
    nord, n1, rows, w = t.shape
    return pl.pallas_call(
        _mid_taps_kernel,
        grid=(n1, nord),
        in_specs=[pl.BlockSpec((1, rows, rows), lambda k, o: (k, 0, 0)),
                  pl.BlockSpec((1, 1, rows, w), lambda k, o: (o, k, 0, 0))],
        out_specs=pl.BlockSpec((1, 1, rows, w), lambda k, o: (o, k, 0, 0)),
        out_shape=jax.ShapeDtypeStruct(t.shape, F32),
        compiler_params=_cp(("parallel", "parallel")),
        name="dft_mid_taps",
    )(g_fwd, t)


def _mid_kernel(gf_ref, gi_ref, k_ref, t_ref, o_ref):
    n2 = FFT_N2
    z = jnp.dot(gf_ref[0], t_ref[0, 0], preferred_element_type=F32)
    zr, zi = z[:n2], z[n2:]
    kr, ki = k_ref[0, :n2], k_ref[0, n2:]
    y = jnp.concatenate([zr * kr - zi * ki, zr * ki + zi * kr], axis=0).astype(BF16)
    o_ref[0, 0] = jnp.dot(gi_ref[0], y, preferred_element_type=F32).astype(o_ref.dtype)


def _mid(t, kspec, g_fwd, g_inv):
    npair, n1, rows, w = t.shape
    gspec = pl.BlockSpec((1, rows, rows), lambda k, p: (k, 0, 0))
    return pl.pallas_call(
        _mid_kernel,
        grid=(n1, npair),
        in_specs=[gspec, gspec,
                  pl.BlockSpec((1, rows, w), lambda k, p: (k, 0, 0)),
                  pl.BlockSpec((1, 1, rows, w), lambda k, p: (p, k, 0, 0))],
        out_specs=pl.BlockSpec((1, 1, rows, w), lambda k, p: (p, k, 0, 0)),
        out_shape=jax.ShapeDtypeStruct(t.shape, BF16),
        compiler_params=_cp(("parallel", "arbitrary")),
        name="dft_mid",
    )(g_fwd, g_inv, kspec, t)


def _last_kernel(m_ref, u_ref, z_ref, x_ref, bias_ref, o_ref):
    y = jnp.dot(m_ref[...], u_ref[0], preferred_element_type=F32)
    conv = (y + z_ref[0].astype(F32) * bias_ref[...]).astype(BF16)
    o_ref[0] = (x_ref[0] * conv).astype(o_ref.dtype)


def _last(u, zv, xv, bias_row, m2):
    npair, n1, ncols = zv.shape
    ch = _fft_chunk(ncols)
    tok = pl.BlockSpec((1, n1, ch), lambda p, j: (p, 0, j))
    bias = jnp.tile(bias_row, (1, ch // bias_row.shape[1]))
    return pl.pallas_call(
        _last_kernel,
        grid=(npair, ncols // ch),
        in_specs=[pl.BlockSpec(m2.shape, lambda p, j: (0, 0)),
                  pl.BlockSpec((1, 2 * n1, ch), lambda p, j: (p, 0, j)),
                  tok, tok,
                  pl.BlockSpec((1, ch), lambda p, j: (0, 0))],
        out_specs=tok,
        out_shape=jax.ShapeDtypeStruct(zv.shape, BF16),
        compiler_params=_cp(("parallel", "parallel")),
        name="dft_last",
    )(m2, u, zv, xv, bias)


def _hyena_spectra(length, p):
    lp = max(length, FFT_MIN_LEN)
    n1 = 2 * lp // FFT_N2
    n = 2 * lp
    w = BRANCH_W
    filt = _hy_filters(length, p).reshape(length, HY_ORDER, 2, w)
    h_fwd = jnp.moveaxis(filt[:, :, 0], 1, 0)
    h_bwd = jnp.moveaxis(filt[:, :, 1], 1, 0)
    taps = jnp.concatenate([h_fwd, jnp.zeros((HY_ORDER, n - 2 * length + 1, w), F32),
                            h_bwd[:, :0:-1]], axis=1)
    m1, m1t, m2, g_fwd, g_inv = _dft_tables(n1)
    t = _stage1_taps(taps.reshape(HY_ORDER, n1, FFT_N2 * w), m1t)
    kspec = _mid_taps(t.reshape(HY_ORDER, n1, 2 * FFT_N2, w), g_fwd)
    return kspec, (m1.astype(BF16), m2.astype(BF16), g_fwd.astype(BF16), g_inv.astype(BF16))


def _hyena(proj, p, kspec, tables):
    bsz, length, _ = proj.shape
    w = BRANCH_W
    m1, m2, g_fwd, g_inv = tables
    n1 = m1.shape[1]
    lp = n1 * FFT_N2 // 2
    uc = _hy_conv(proj, p)
    if lp != length:
        uc = jnp.pad(uc, ((0, 0), (0, 0), (0, lp - length), (0, 0)))
    view = lambda a: a.reshape(bsz // 2, n1, FFT_N2 * w)
    z, x1, x2 = view(uc[0]), view(uc[1]), view(uc[2])
    for order, mult in enumerate((x1, x2)):
        t = _stage1_data(z, m1)
        u = _mid(t.reshape(bsz // 2, n1, 2 * FFT_N2, w), kspec[order], g_fwd, g_inv)
        z = _last(u.reshape(bsz // 2, 2 * n1, FFT_N2 * w), z, mult,
                  p["hy_bias"][order:order + 1], m2)
    return z.reshape(bsz, lp, w)[:, :length]


def _merge_kernel(x_ref, ya_ref, yb_ref, yc_ref, g0_ref, g1_ref, g2_ref, wb_ref, wo_ref,
                  sn_ref, n1_ref, gate_ref, o_ref):
    yc = _rms(yc_ref[0].astype(F32), sn_ref[...]).astype(BF16)
    merged = None
    for k, (y, g_ref) in enumerate(((ya_ref[0], g0_ref), (yb_ref[0], g1_ref), (yc, g2_ref))):
        term = jax.nn.sigmoid(g_ref[0].astype(F32)) * jnp.dot(y, wb_ref[k],
                                                              preferred_element_type=F32)
        merged = term if merged is None else merged + term
    out = jnp.dot(merged.astype(BF16), wo_ref[...], preferred_element_type=F32)
    o_ref[0] = x_ref[0] + gate_ref[0] * _rms(out, n1_ref[...])


def _merge(x, proj, ya, yb, yc, p, norm1, mod, row_of, k_gate):
    bsz, length, _ = x.shape
    tm = min(512, length)
    w = BRANCH_W
    tok = lambda c: pl.BlockSpec((1, tm, c), lambda b, i: (b, i, 0))
    gspec = lambda k: pl.BlockSpec((1, tm, D_MODEL),
                                   lambda b, i: (b, i, COL_GATES // D_MODEL + k))
    return pl.pallas_call(
        _merge_kernel,
        grid=(bsz, length // tm),
        in_specs=[tok(D_MODEL), tok(w), tok(w), tok(w), gspec(0), gspec(1), gspec(2),
                  pl.BlockSpec((N_BRANCH, w, D_MODEL), lambda b, i: (0, 0, 0)),
                  pl.BlockSpec((D_MODEL, D_MODEL), lambda b, i: (0, 0)),
                  pl.BlockSpec((1, w), lambda b, i: (0, 0)),
                  pl.BlockSpec((1, D_MODEL), lambda b, i: (0, 0)),
                  pl.BlockSpec((1, 1, D_MODEL), lambda b, i: (row_of(b), 0, k_gate))],
        out_specs=tok(D_MODEL),
        out_shape=jax.ShapeDtypeStruct(x.shape, F32),
        compiler_params=_cp(("parallel", "parallel"), VMEM_BIG),
        name="merge_out",
    )(x, ya, yb, yc, proj, proj, proj, p["w_branch"], p["w_out"], p["ssm_norm"],
      norm1.reshape(1, D_MODEL), mod)


FFN_CH = 256


def _ffn_tail_kernel(x_ref, u_ref, up_ref, un_ref, cw_ref, cb_ref, wd_ref, n3_ref, gate_ref,
                     o_ref, *, grid_w, tm):
    i = pl.program_id(1)
    nt = pl.num_programs(1)
    has_prev = (i > 0).astype(F32)
    has_next = (i < nt - 1).astype(F32)
    rows = tm + 2 * grid_w
    col = lax.broadcasted_iota(jnp.int32, (rows, FFN_CH), 0) % grid_w
    not_first = col != 0
    not_last = col != grid_w - 1

    def conv(c0):
        ext = jnp.concatenate([up_ref[0, :, c0:c0 + FFN_CH].astype(F32) * has_prev,
                               u_ref[0, :, c0:c0 + FFN_CH].astype(F32),
                               un_ref[0, :, c0:c0 + FFN_CH].astype(F32) * has_next], axis=0)
        left = jnp.where(not_first, pltpu.roll(ext, 1, 0), 0.0)
        right = jnp.where(not_last, pltpu.roll(ext, rows - 1, 0), 0.0)
        acc = cb_ref[:, c0:c0 + FFN_CH]
        for dr in range(3):
            base = dr * grid_w
            for dc, src in enumerate((left, ext, right)):
                acc = acc + cw_ref[dr * 3 + dc:dr * 3 + dc + 1, c0:c0 + FFN_CH] * src[base:base + tm]
        return acc

    acc = None
    for j in range(FFN_HIDDEN // FFN_CH):
        val = conv(j * FFN_CH)
        gate = conv(FFN_HIDDEN + j * FFN_CH)
        h = (jax.nn.gelu(gate) * val).astype(BF16)
        part = jnp.dot(h, wd_ref[j * FFN_CH:(j + 1) * FFN_CH, :], preferred_element_type=F32)
        acc = part if acc is None else acc + part
    o_ref[0] = x_ref[0] + gate_ref[0] * _rms(acc, n3_ref[...])


def _ffn_tail(x, u, p, norm3, mod, row_of, k_gate, grid_w):
    bsz, length, _ = x.shape
    tm = min(512, length)
    ncol = 2 * FFN_HIDDEN
    per = tm // grid_w
    nblk = length // grid_w
    return pl.pallas_call(
        functools.partial(_ffn_tail_kernel, grid_w=grid_w, tm=tm),
        grid=(bsz, length // tm),
        in_specs=[pl.BlockSpec((1, tm, D_MODEL), lambda b, i: (b, i, 0)),
                  pl.BlockSpec((1, tm, ncol), lambda b, i: (b, i, 0)),
                  pl.BlockSpec((1, grid_w, ncol), lambda b, i: (b, jnp.maximum(i * per - 1, 0), 0)),
                  pl.BlockSpec((1, grid_w, ncol),
                               lambda b, i: (b, jnp.minimum((i + 1) * per, nblk - 1), 0)),
                  pl.BlockSpec((9, ncol), lambda b, i: (0, 0)),
                  pl.BlockSpec((1, ncol), lambda b, i: (0, 0)),
                  pl.BlockSpec((FFN_HIDDEN, D_MODEL), lambda b, i: (0, 0)),
                  pl.BlockSpec((1, D_MODEL), lambda b, i: (0, 0)),
                  pl.BlockSpec((1, 1, D_MODEL), lambda b, i: (row_of(b), 0, k_gate))],
        out_specs=pl.BlockSpec((1, tm, D_MODEL), lambda b, i: (b, i, 0)),
        out_shape=jax.ShapeDtypeStruct(x.shape, F32),
        compiler_params=_cp(("parallel", "parallel"), VMEM_BIG),
        name="ffn_tail",
    )(x, u, u, u, p["ffn_conv_w"], p["ffn_conv_b"], p["ffn_w_down"],
      norm3.reshape(1, D_MODEL), mod)


def _block_diag(w):
    nb, k, j = w.shape
    eye = jnp.eye(nb, dtype=w.dtype)
    return (eye[:, None, :, None] * w[:, :, None, :]).reshape(nb * k, nb * j)


def _layer_params(l, a):
    ns = SSM_STATE
    w_in = a["w_in"][l]
    xbc0 = BRANCH_W
    b0 = xbc0 + BRANCH_W
    c0 = b0 + SSM_BC
    bc_cols = jnp.concatenate([w_in[:, b0:b0 + ns], w_in[:, c0:c0 + ns],
                               w_in[:, b0 + ns:b0 + 2 * ns], w_in[:, c0 + ns:c0 + 2 * ns]], axis=1)
    out0 = STATE_COLS
    w_main = jnp.concatenate([
        w_in[:, :BRANCH_W],
        w_in[:, xbc0:xbc0 + BRANCH_W],
        bc_cols,
        w_in[:, out0:out0 + BRANCH_W],
        w_in[:, out0 + BRANCH_W:out0 + BRANCH_W + HY_PROJ],
        w_in[:, out0 + BRANCH_W + HY_PROJ:out0 + 2 * BRANCH_W + HY_PROJ],
        w_in[:, out0 + 2 * BRANCH_W + HY_PROJ:],
    ], axis=1).astype(BF16)
    dt0 = BRANCH_W + SSM_XBC

    def per_group(v2):
        rows = []
        for g in range(SSM_GROUPS):
            sel = v2[:, g * HEADS_PER_GROUP:(g + 1) * HEADS_PER_GROUP].reshape(-1)
            rows.append(jnp.pad(sel, (0, LANES - sel.shape[0])))
        return jnp.concatenate(rows)[None, :]

    w_dt_cols = w_in[:, dt0:dt0 + 2 * SSM_HEADS].reshape(D_MODEL, 2, SSM_HEADS)
    w_dt = jnp.concatenate([
        jnp.pad(w_dt_cols[:, :, g * HEADS_PER_GROUP:(g + 1) * HEADS_PER_GROUP].reshape(D_MODEL, -1),
                ((0, 0), (0, LANES - 2 * HEADS_PER_GROUP))) for g in range(SSM_GROUPS)],
        axis=1).astype(BF16)
    scw = a["ssm_conv_w"][l]
    scb = a["ssm_conv_b"][l][None, :]
    perm_bc = lambda v: jnp.concatenate(
        [v[:, BRANCH_W:BRANCH_W + ns], v[:, BRANCH_W + SSM_BC:BRANCH_W + SSM_BC + ns],
         v[:, BRANCH_W + ns:BRANCH_W + 2 * ns], v[:, BRANCH_W + SSM_BC + ns:]], axis=1)
    return {
        "w_main": w_main, "w_dt": w_dt,
        "lru_conv_w": a["lru_conv_w"][l], "lru_conv_b": a["lru_conv_b"][l][None, :],
        "lru_waf": _block_diag(a["lru_w_a"][l, 0]).astype(BF16),
        "lru_wif": _block_diag(a["lru_w_i"][l, 0]).astype(BF16),
        "lru_wab": _block_diag(a["lru_w_a"][l, 1]).astype(BF16),
        "lru_wib": _block_diag(a["lru_w_i"][l, 1]).astype(BF16),
        "lru_bias": jnp.stack([a["lru_b_a"][l, 0], a["lru_b_i"][l, 0],
                               a["lru_b_a"][l, 1], a["lru_b_i"][l, 1]]),
        "lru_lambda": a["lru_lambda"][l],
        "hy_conv_w": a["hy_conv_w"][l], "hy_conv_b": a["hy_conv_b"][l][None, :],
        "hy_w1": jnp.pad(a["hy_w1"][l], ((0, LANES - HY_EMB), (0, 0))),
        "hy_b1": a["hy_b1"][l][None, :], "hy_w2": a["hy_w2"][l], "hy_b2": a["hy_b2"][l][None, :],
        "hy_w3": a["hy_w3"][l], "hy_freq": a["hy_freq"][l], "hy_bias": a["hy_bias"][l],
        "ssm_cw_x": scw[:, :BRANCH_W], "ssm_cb_x": scb[:, :BRANCH_W],
        "ssm_cw_bc": perm_bc(scw), "ssm_cb_bc": perm_bc(scb),
        "ssm_dtb": per_group(a["ssm_dt_bias"][l]), "ssm_alog": per_group(a["ssm_a_log"][l]),
        "ssm_dch": jnp.repeat(a["ssm_d"][l], SSM_HEADDIM)[None, :],
        "ssm_norm": a["ssm_norm"][l][None, :],
        "w_branch": a["w_branch"][l].astype(BF16), "w_out": a["w_out"][l].astype(BF16),
        "ffn_w_up": a["ffn_w_up"][l].astype(BF16),
        "ffn_conv_w": a["ffn_conv_w"][l].reshape(9, 2 * FFN_HIDDEN),
        "ffn_conv_b": a["ffn_conv_b"][l][None, :],
        "ffn_w_down": a["ffn_w_down"][l].astype(BF16),
    }


def _mixer_states(tokens, p, norm0, mod, row_of, base, init):
    proj, dt = _norm_mm(tokens, norm0, mod, row_of, base, base + 1, p["w_main"], p["w_dt"])
    ya, lru_fin = _lru(proj, p, init[0])
    yc, ssm_fin = _ssd(proj, dt, p, init[1])
    return proj, ya, yc, (lru_fin, ssm_fin)


def _layer_tokens(tokens, p, norms_l, mod, row_of, init, hy, grid_w):
    proj, ya, yc, finals = _mixer_states(tokens, p, norms_l[0], mod, row_of, 0, init)
    yb = _hyena(proj, p, *hy)
    tokens = _merge(tokens, proj, ya, yb, yc, p, norms_l[1], mod, row_of, 2)
    u = _norm_mm(tokens, norms_l[2], mod, row_of, 3, 4, p["ffn_w_up"], tn=512)
    tokens = _ffn_tail(tokens, u, p, norms_l[3], mod, row_of, 5, grid_w)
    return tokens, finals


def kernel(x, c, ctx, c_ctx, mod_w, mod_b, norms, w_in, lru_conv_w, lru_conv_b, lru_w_a, lru_b_a,
           lru_w_i, lru_b_i, lru_lambda, hy_conv_w, hy_conv_b, hy_w1, hy_b1, hy_w2, hy_b2, hy_w3,
           hy_freq, hy_bias, ssm_conv_w, ssm_conv_b, ssm_dt_bias, ssm_a_log, ssm_d, ssm_norm,
           w_branch, w_out, ffn_w_up, ffn_conv_w, ffn_conv_b, ffn_w_down):
    a = dict(w_in=w_in, lru_conv_w=lru_conv_w, lru_conv_b=lru_conv_b, lru_w_a=lru_w_a,
             lru_b_a=lru_b_a, lru_w_i=lru_w_i, lru_b_i=lru_b_i, lru_lambda=lru_lambda,
             hy_conv_w=hy_conv_w, hy_conv_b=hy_conv_b, hy_w1=hy_w1, hy_b1=hy_b1, hy_w2=hy_w2,
             hy_b2=hy_b2, hy_w3=hy_w3, hy_freq=hy_freq, hy_bias=hy_bias, ssm_conv_w=ssm_conv_w,
             ssm_conv_b=ssm_conv_b, ssm_dt_bias=ssm_dt_bias, ssm_a_log=ssm_a_log, ssm_d=ssm_d,
             ssm_norm=ssm_norm, w_branch=w_branch, w_out=w_out, ffn_w_up=ffn_w_up,
             ffn_conv_w=ffn_conv_w, ffn_conv_b=ffn_conv_b, ffn_w_down=ffn_w_down)
    bsz, length, _ = x.shape
    ctx_len = ctx.shape[1]
    depth = mod_w.shape[0]
    rows = -(-(bsz + 1) // 8) * 8
    cc = jnp.concatenate([c, c_ctx[None, :], jnp.zeros((rows - bsz - 1, D_MODEL), F32)], axis=0)
    mod_all = _modulation(cc, mod_w, mod_b)
    lat_row = lambda b: b
    ctx_row = lambda b: bsz
    zero_init = (jnp.zeros((bsz, 2, BRANCH_W), F32),
                 jnp.zeros((bsz, 2, SSM_HEADS, SSM_STATE, SSM_HEADDIM), F32))
    for l in range(depth):
        last = l == depth - 1
        p = _layer_params(l, a)
        mod = mod_all[l].reshape(rows, 1, 6 * D_MODEL)
        if last:
            _, _, _, ctx_states = _mixer_states(ctx, p, norms[l, 0], mod, ctx_row, 0, zero_init)
        else:
            hy_ctx = _hyena_spectra(ctx_len, p)
            ctx, ctx_states = _layer_tokens(ctx, p, norms[l], mod, ctx_row, zero_init, hy_ctx,
                                            ctx_len)
        hy_lat = _hyena_spectra(length, p)
        x, _ = _layer_tokens(x, p, norms[l], mod, lat_row, ctx_states, hy_lat, GRID_W)
    return x
```

```python
import functools
import math

import jax
import jax.numpy as jnp
from jax import lax
from jax.experimental import pallas as pl
from jax.experimental.pallas import tpu as pltpu

F32 = jnp.float32
BF16 = jnp.bfloat16
HIGHEST = lax.Precision.HIGHEST

D_MODEL = 1024
DEPTH = 2
GRID_W = 64
BRANCH_W = 512
N_BRANCH = 3
LRU_BLOCKS = 8
LRU_BLOCK_W = BRANCH_W // LRU_BLOCKS
LRU_C = 8.0
HY_ORDER = 2
HY_PROJ = (HY_ORDER + 1) * BRANCH_W
HY_BANDS = 16
HY_EMB = 1 + 2 * HY_BANDS
HY_FFN = 64
HY_TARGET = 1e-2
HY_FAST_DECAY = 0.3
HY_SLOW_DECAY = 1.5
SSM_HEADDIM = 64
SSM_HEADS = BRANCH_W // SSM_HEADDIM
SSM_GROUPS = 2
SSM_STATE = 128
SSM_CHUNK = 128
SSM_BC = SSM_GROUPS * SSM_STATE
SSM_XBC = BRANCH_W + 2 * SSM_BC
FFN_HIDDEN = 2816
RMS_EPS = 1e-6
STATE_COLS = BRANCH_W + SSM_XBC + 2 * SSM_HEADS

COL_XA = 0
COL_XS = 512
COL_BC = 1024
COL_GA = 1536
COL_HY = 2048
COL_Z = 3584
COL_GATES = 4096
MAIN_COLS = 7168
HEADS_PER_GROUP = SSM_HEADS // SSM_GROUPS
GROUP_W = BRANCH_W // SSM_GROUPS
LANES = 128
FFT_N2 = 128
FFT_MIN_LEN = 1024
VMEM_BIG = 56 * 1024 * 1024


def _cp(sem, vmem=None):
    return pltpu.CompilerParams(dimension_semantics=sem, vmem_limit_bytes=vmem)


def _rms(u, g):
    return u * lax.rsqrt(jnp.mean(u * u, axis=-1, keepdims=True) + RMS_EPS) * g


def _softplus(x):
    return jnp.maximum(x, 0.0) + jnp.log1p(jnp.exp(-jnp.abs(x)))


def _mod_kernel(c_ref, w_ref, b_ref, o_ref):
    c = c_ref[...]
    o_ref[0] = jnp.dot(jax.nn.silu(c), w_ref[0], precision=HIGHEST,
                       preferred_element_type=F32) + b_ref[0]


def _modulation(cc, mod_w, mod_b):
    rows = cc.shape[0]
    depth, _, ncol = mod_w.shape
    tn = 1024
    return pl.pallas_call(
        _mod_kernel,
        grid=(depth, ncol // tn),
        in_specs=[pl.BlockSpec((rows, D_MODEL), lambda l, j: (0, 0)),
                  pl.BlockSpec((1, D_MODEL, tn), lambda l, j: (l, 0, j)),
                  pl.BlockSpec((1, 1, tn), lambda l, j: (l, 0, j))],
        out_specs=pl.BlockSpec((1, rows, tn), lambda l, j: (l, 0, j)),
        out_shape=jax.ShapeDtypeStruct((depth, rows, ncol), F32),
        compiler_params=_cp(("parallel", "parallel")),
        name="modulation",
    )(cc, mod_w, mod_b.reshape(depth, 1, ncol))


def _norm_mm_kernel(x_ref, g_ref, sh_ref, sc_ref, w_ref, *rest, has_dt):
    if has_dt:
        wdt_ref, o_ref, odt_ref, hx_ref = rest
    else:
        o_ref, hx_ref = rest

    @pl.when(pl.program_id(2) == 0)
    def _():
        h = _rms(x_ref[0], g_ref[...]) * (1.0 + sc_ref[0]) + sh_ref[0]
        hx_ref[...] = h.astype(BF16)
        if has_dt:
            odt_ref[0] = jnp.dot(hx_ref[...], wdt_ref[...], preferred_element_type=F32)

    o_ref[0] = jnp.dot(hx_ref[...], w_ref[...], preferred_element_type=F32).astype(o_ref.dtype)


def _norm_mm(x, gnorm, mod, row_of, k_shift, k_scale, w, w_dt=None, tn=1024):
    bsz, length, _ = x.shape
    ncol = w.shape[1]
    tm = min(1024, length)
    has_dt = w_dt is not None
    in_specs = [pl.BlockSpec((1, tm, D_MODEL), lambda b, i, j: (b, i, 0)),
                pl.BlockSpec((1, D_MODEL), lambda b, i, j: (0, 0)),
                pl.BlockSpec((1, 1, D_MODEL), lambda b, i, j: (row_of(b), 0, k_shift)),
                pl.BlockSpec((1, 1, D_MODEL), lambda b, i, j: (row_of(b), 0, k_scale)),
                pl.BlockSpec((D_MODEL, tn), lambda b, i, j: (0, j))]
    args = [x, gnorm.reshape(1, D_MODEL), mod, mod, w]
    out_specs = [pl.BlockSpec((1, tm, tn), lambda b, i, j: (b, i, j))]
    out_shape = [jax.ShapeDtypeStruct((bsz, length, ncol), BF16)]
    if has_dt:
        ndt = w_dt.shape[1]
        in_specs.append(pl.BlockSpec((D_MODEL, ndt), lambda b, i, j: (0, 0)))
        args.append(w_dt)
        out_specs.append(pl.BlockSpec((1, tm, ndt), lambda b, i, j: (b, i, 0)))
        out_shape.append(jax.ShapeDtypeStruct((bsz, length, ndt), F32))
    res = pl.pallas_call(
        functools.partial(_norm_mm_kernel, has_dt=has_dt),
        grid=(bsz, length // tm, ncol // tn),
        in_specs=in_specs, out_specs=out_specs, out_shape=out_shape,
        scratch_shapes=[pltpu.VMEM((tm, D_MODEL), BF16)],
        compiler_params=_cp(("parallel", "parallel", "arbitrary"), VMEM_BIG),
        name="norm_proj",
    )(*args)
    return res if has_dt else res[0]


PAD = 8


def _fill_padded(pad_ref, src_ref, length, width, chunk):
    zeros = jnp.zeros((PAD, width), F32)
    pad_ref[pl.ds(0, PAD), :] = zeros
    pad_ref[pl.ds(PAD + length, PAD), :] = zeros

    def body(i, carry):
        t0 = pl.multiple_of(i * chunk, chunk)
        pad_ref[pl.ds(PAD + t0, chunk), :] = src_ref[0, pl.ds(t0, chunk), :].astype(F32)
        return carry

    lax.fori_loop(0, length // chunk, body, 0)


def _conv_rows(pad_ref, t0, rows, w, b, left):
    win = pad_ref[pl.ds(t0, rows + 2 * PAD), :]
    acc = b
    for k in range(w.shape[0]):
        off = PAD + k - left
        acc = acc + w[k:k + 1, :] * win[off:off + rows, :]
    return acc


LRU_CHUNK = 128


SUBLANES = 8


def _chunk_scan(a, b, h, reverse):
    rows, width = a.shape
    groups = rows // SUBLANES
    a3 = a.reshape(groups, SUBLANES, width)
    b3 = b.reshape(groups, SUBLANES, width)
    sub = lax.broadcasted_iota(jnp.int32, a3.shape, 1)
    s = 1
    while s < SUBLANES:
        keep = sub < SUBLANES - s if reverse else sub >= s
        shift = SUBLANES - s if reverse else s
        a_s = jnp.where(keep, pltpu.roll(a3, shift, 1), 1.0)
        b_s = jnp.where(keep, pltpu.roll(b3, shift, 1), 0.0)
        b3 = a3 * b_s + b3
        a3 = a3 * a_s
        s *= 2
    out = [None] * groups
    for i in (range(groups - 1, -1, -1) if reverse else range(groups)):
        hs = a3[i] * h + b3[i]
        out[i] = hs
        h = hs[0:1] if reverse else hs[SUBLANES - 1:SUBLANES]
    return jnp.concatenate(out, axis=0), h


def _lru_kernel(xa_ref, ga_ref, cw_ref, cb_ref, waf_ref, wif_ref, wab_ref, wib_ref,
                bias_ref, lam_ref, h0_ref, y_ref, fin_ref, pad_ref, hf_ref, *, length):
    width = xa_ref.shape[-1]
    tc = LRU_CHUNK
    nchunk = length // tc
    _fill_padded(pad_ref, xa_ref, length, width, tc)
    cw = cw_ref[...]
    cb = cb_ref[...]
    neg_c_sp = -LRU_C * _softplus(-lam_ref[...])

    def gated(t0, wa_ref, wi_ref, d):
        xc = _conv_rows(pad_ref, t0, tc, cw, cb, 2)
        xcb = xc.astype(BF16)
        r = jax.nn.sigmoid(jnp.dot(xcb, wa_ref[...], preferred_element_type=F32)
                           + bias_ref[2 * d:2 * d + 1, :])
        i = jax.nn.sigmoid(jnp.dot(xcb, wi_ref[...], preferred_element_type=F32)
                           + bias_ref[2 * d + 1:2 * d + 2, :])
        log_a = neg_c_sp[d:d + 1, :] * r
        a = jnp.exp(log_a)
        gx = jnp.sqrt(jnp.tanh(-log_a) * (1.0 + a * a)) * (i * xc)
        return a, gx

    def fwd_body(c, h):
        t0 = pl.multiple_of(c * tc, tc)
        a, gx = gated(t0, waf_ref, wif_ref, 0)
        hs, h = _chunk_scan(a, gx, h, False)
        hf_ref[pl.ds(t0, tc), :] = hs
        return h

    h_f = lax.fori_loop(0, nchunk, fwd_body, h0_ref[0, 0:1, :])
    fin_ref[0, 0:1, :] = h_f

    def bwd_body(c, h):
        t0 = pl.multiple_of((nchunk - 1 - c) * tc, tc)
        a, gx = gated(t0, wab_ref, wib_ref, 1)
        hs, h = _chunk_scan(a, gx, h, True)
        ga = ga_ref[0, pl.ds(t0, tc), :].astype(F32)
        y = (hf_ref[pl.ds(t0, tc), :] + hs) * jax.nn.gelu(ga)
        y_ref[0, pl.ds(t0, tc), :] = y.astype(y_ref.dtype)
        return h

    h_b = lax.fori_loop(0, nchunk, bwd_body, h0_ref[0, 1:2, :])
    fin_ref[0, 1:2, :] = h_b


def _lru(proj, p, h0):
    bsz, length, _ = proj.shape
    cw_ = GROUP_W
    nsplit = BRANCH_W // cw_
    blk = lambda off: pl.BlockSpec((1, length, cw_), lambda b, j: (b, 0, off // cw_ + j))
    vec = lambda rows: pl.BlockSpec((rows, cw_), lambda b, j: (0, j))
    wspec = pl.BlockSpec((cw_, cw_), lambda b, j: (j, j))
    st = pl.BlockSpec((1, 2, cw_), lambda b, j: (b, 0, j))
    return pl.pallas_call(
        functools.partial(_lru_kernel, length=length),
        grid=(bsz, nsplit),
        in_specs=[blk(COL_XA), blk(COL_GA), vec(4), vec(1), wspec, wspec, wspec, wspec,
                  vec(4), vec(2), st],
        out_specs=[pl.BlockSpec((1, length, cw_), lambda b, j: (b, 0, j)), st],
        out_shape=[jax.ShapeDtypeStruct((bsz, length, BRANCH_W), BF16),
                   jax.ShapeDtypeStruct((bsz, 2, BRANCH_W), F32)],
        scratch_shapes=[pltpu.VMEM((length + 2 * PAD, cw_), F32),
                        pltpu.VMEM((length, cw_), F32)],
        compiler_params=_cp(("parallel", "parallel"), VMEM_BIG),
        name="rglru",
    )(proj, proj, p["lru_conv_w"], p["lru_conv_b"], p["lru_waf"], p["lru_wif"],
      p["lru_wab"], p["lru_wib"], p["lru_bias"], p["lru_lambda"], h0)


def _ssd_kernel(xs_ref, bc_ref, z_ref, dt_ref, cwx_ref, cbx_ref, cwb_ref, cbb_ref,
                dtb_ref, alog_ref, dch_ref, tri_ref, ex_ref, h0_ref, y_ref, fin_ref,
                padx_ref, padb_ref, xc_ref, yf_ref, st_ref, *, length):
    q = SSM_CHUNK
    nchunk = length // q
    gw = xs_ref.shape[-1]
    nh = HEADS_PER_GROUP
    hp = SSM_HEADDIM
    ns = SSM_STATE
    _fill_padded(padx_ref, xs_ref, length, gw, q)
    _fill_padded(padb_ref, bc_ref, length, 2 * ns, q)
    cwx, cbx, cwb, cbb = cwx_ref[...], cbx_ref[...], cwb_ref[...], cbb_ref[...]

    def conv_body(c, carry):
        t0 = pl.multiple_of(c * q, q)
        xc_ref[pl.ds(t0, q), 0:gw] = jax.nn.silu(_conv_rows(padx_ref, t0, q, cwx, cbx, 2)).astype(BF16)
        xc_ref[pl.ds(t0, q), gw:gw + 2 * ns] = jax.nn.silu(
            _conv_rows(padb_ref, t0, q, cwb, cbb, 2)).astype(BF16)
        return carry

    lax.fori_loop(0, nchunk, conv_body, 0)

    dtb = dtb_ref[...]
    a_neg = -jnp.exp(alog_ref[...])
    rid = lax.broadcasted_iota(jnp.int32, (q, q), 0)
    cid = lax.broadcasted_iota(jnp.int32, (q, q), 1)
    lower = rid >= cid
    upper = rid <= cid
    lane_head = lax.broadcasted_iota(jnp.int32, (q, gw), 1) // hp

    def chunk(t0, d):
        keep, edge = (lower, q - 1) if d == 0 else (upper, 0)
        dt = _softplus(dt_ref[0, pl.ds(t0, q), :] + dtb)
        cs = jnp.dot(tri_ref[d], jnp.concatenate(_split3(dt * a_neg), axis=0),
                     preferred_element_type=F32)
        cs_t = cs.T
        dt_x = jnp.dot(jnp.concatenate(_split3(dt)[:2], axis=1), ex_ref[d, :, 0:gw],
                       preferred_element_type=F32)
        cs_all = jnp.dot(jnp.concatenate(_split3(cs)[:2], axis=1), ex_ref[d],
                         preferred_element_type=F32)
        cs_x = cs_all[:, 0:gw]
        xs = xc_ref[pl.ds(t0, q), 0:gw].astype(F32)
        bm = xc_ref[pl.ds(t0, q), gw:gw + ns]
        cm = xc_ref[pl.ds(t0, q), gw + ns:gw + 2 * ns]
        bm_t = bm.astype(F32).T.astype(BF16)
        cb = jnp.dot(cm, bm_t, preferred_element_type=F32)
        xdt = xs * dt_x
        xdt_b = xdt.astype(BF16)
        mats = []
        for j in range(nh):
            lane = d * nh + j
            seg = cs_all[:, gw + j * q:gw + (j + 1) * q] - cs_t[lane:lane + 1, :]
            mats.append((cb * jnp.exp(jnp.where(keep, seg, -jnp.inf))).astype(BF16))
        y_all = jnp.dot(jnp.concatenate(mats, axis=0), xdt_b, preferred_element_type=F32)
        y_diag = y_all[(nh - 1) * q:nh * q]
        for j in range(nh - 2, -1, -1):
            y_diag = jnp.where(lane_head == j, y_all[j * q:(j + 1) * q], y_diag)
        st = st_ref[...]
        y_off = jnp.dot(cm, st.astype(BF16), preferred_element_type=F32) * jnp.exp(cs_x)
        cs_edge = cs_x[edge:edge + 1, :]
        xw = (xdt * jnp.exp(cs_edge - cs_x)).astype(BF16)
        st_ref[...] = jnp.exp(cs_edge) * st + jnp.dot(bm_t, xw, preferred_element_type=F32)
        return y_diag + y_off, xs

    st_ref[...] = h0_ref[0, 0, 0]

    def fwd_body(c, carry):
        t0 = pl.multiple_of(c * q, q)
        y, _ = chunk(t0, 0)
        yf_ref[pl.ds(t0, q), :] = y
        return carry

    unroll = 2 if nchunk % 2 == 0 else 1
    lax.fori_loop(0, nchunk, fwd_body, 0, unroll=unroll)
    fin_ref[0, 0, 0] = st_ref[...]
    st_ref[...] = h0_ref[0, 1, 0]
    dch = dch_ref[...]

    def bwd_body(c, carry):
        t0 = pl.multiple_of((nchunk - 1 - c) * q, q)
        y, xs = chunk(t0, 1)
        ys = yf_ref[pl.ds(t0, q), :] + y + dch * xs
        z = z_ref[0, pl.ds(t0, q), :].astype(F32)
        y_ref[0, pl.ds(t0, q), :] = (ys * jax.nn.silu(z)).astype(y_ref.dtype)
        return carry

    lax.fori_loop(0, nchunk, bwd_body, 0, unroll=unroll)
    fin_ref[0, 1, 0] = st_ref[...]


def _split3(x):
    hi = x.astype(BF16)
    rest = x - hi.astype(F32)
    mid = rest.astype(BF16)
    return hi, mid, (rest - mid.astype(F32)).astype(BF16)


def _ssd_tables():
    q, nh, hp, gw = SSM_CHUNK, HEADS_PER_GROUP, SSM_HEADDIM, GROUP_W
    r = jnp.arange(q)
    lower = (r[:, None] >= r[None, :]).astype(BF16)
    tri = jnp.stack([jnp.tile(lower, (1, 3)), jnp.tile(lower.T, (1, 3))])
    src = jnp.arange(LANES)[:, None]
    col = jnp.arange(gw + nh * q)[None, :]
    head_of_col = jnp.where(col < gw, col // hp, (col - gw) // q)
    expand = jnp.stack([jnp.tile((src == d * nh + head_of_col).astype(BF16), (2, 1))
                        for d in range(2)])
    return tri, expand


def _ssd(proj, dt, p, h0):
    bsz, length, _ = proj.shape
    gw, ns, nh, hp = GROUP_W, SSM_STATE, HEADS_PER_GROUP, SSM_HEADDIM
    tri, expand = _ssd_tables()
    blk = lambda off: pl.BlockSpec((1, length, gw), lambda b, g: (b, 0, off // gw + g))
    vec = lambda rows, w: pl.BlockSpec((rows, w), lambda b, g: (0, g))
    full = lambda a: pl.BlockSpec(a.shape, lambda b, g: (0,) * a.ndim)
    st = pl.BlockSpec((1, 2, 1, ns, gw), lambda b, g: (b, 0, g, 0, 0))
    return pl.pallas_call(
        functools.partial(_ssd_kernel, length=length),
        grid=(bsz, SSM_GROUPS),
        in_specs=[blk(COL_XS), blk(COL_BC), blk(COL_Z),
                  pl.BlockSpec((1, length, LANES), lambda b, g: (b, 0, g)),
                  vec(4, gw), vec(1, gw), vec(4, 2 * ns), vec(1, 2 * ns),
                  vec(1, LANES), vec(1, LANES), vec(1, gw), full(tri), full(expand), st],
        out_specs=[pl.BlockSpec((1, length, gw), lambda b, g: (b, 0, g)), st],
        out_shape=[jax.ShapeDtypeStruct((bsz, length, BRANCH_W), BF16),
                   jax.ShapeDtypeStruct((bsz, 2, SSM_GROUPS, ns, gw), F32)],
        scratch_shapes=[pltpu.VMEM((length + 2 * PAD, gw), F32),
                        pltpu.VMEM((length + 2 * PAD, 2 * ns), F32),
                        pltpu.VMEM((length, gw + 2 * ns), BF16),
                        pltpu.VMEM((length, gw), F32),
                        pltpu.VMEM((ns, gw), F32)],
        compiler_params=_cp(("parallel", "parallel"), VMEM_BIG),
        name="ssd",
    )(proj, proj, proj, dt, p["ssm_cw_x"], p["ssm_cb_x"], p["ssm_cw_bc"], p["ssm_cb_bc"],
      p["ssm_dtb"], p["ssm_alog"], p["ssm_dch"], tri, expand, h0)


HY_CONV_CHUNK = 256


def _hy_conv_kernel(u_ref, w_ref, b_ref, o_ref, pad_ref, *, length):
    width = u_ref.shape[-1]
    tc = min(HY_CONV_CHUNK, length)
    _fill_padded(pad_ref, u_ref, length, width, tc)
    w, b = w_ref[...], b_ref[...]

    def body(c, carry):
        t0 = pl.multiple_of(c * tc, tc)
        o_ref[0, 0, pl.ds(t0, tc), :] = _conv_rows(pad_ref, t0, tc, w, b, 1).astype(o_ref.dtype)
        return carry

    lax.fori_loop(0, length // tc, body, 0)


def _hy_conv(proj, p):
    bsz, length, _ = proj.shape
    w = BRANCH_W
    return pl.pallas_call(
        functools.partial(_hy_conv_kernel, length=length),
        grid=(bsz, 3),
        in_specs=[pl.BlockSpec((1, length, w), lambda b, j: (b, 0, COL_HY // w + j)),
                  pl.BlockSpec((3, w), lambda b, j: (0, j)),
                  pl.BlockSpec((1, w), lambda b, j: (0, j))],
        out_specs=pl.BlockSpec((1, 1, length, w), lambda b, j: (j, b, 0, 0)),
        out_shape=jax.ShapeDtypeStruct((3, bsz, length, w), BF16),
        scratch_shapes=[pltpu.VMEM((length + 2 * PAD, w), F32)],
        compiler_params=_cp(("parallel", "parallel"), VMEM_BIG),
        name="hyena_short_conv",
    )(proj, p["hy_conv_w"], p["hy_conv_b"])


def _hy_filter_kernel(f_ref, w1_ref, b1_ref, w2_ref, b2_ref, w3_ref, fr_ref, dl_ref, o_ref):
    feats = f_ref[...]
    h = jnp.sin(fr_ref[0:1, :] * (jnp.dot(feats, w1_ref[...], precision=HIGHEST,
                                          preferred_element_type=F32) + b1_ref[...]))
    h = jnp.sin(fr_ref[1:2, :] * (jnp.dot(h, w2_ref[...], precision=HIGHEST,
                                          preferred_element_type=F32) + b2_ref[...]))
    h = jnp.dot(h, w3_ref[...], precision=HIGHEST, preferred_element_type=F32)
    window = jnp.exp(-feats[:, 0:1] * dl_ref[...])
    o_ref[...] = h * jnp.concatenate([window] * (2 * HY_ORDER), axis=1)


def _hy_filters(length, p):
    pos = jnp.arange(length, dtype=F32)[:, None]
    t01 = jnp.linspace(0.0, 1.0, length, dtype=F32)[:, None]
    bands = jnp.linspace(1e-4, HY_BANDS - 1, HY_BANDS, dtype=F32)
    ang = (2.0 * math.pi / length) * pos * bands
    feats = jnp.concatenate([t01, jnp.cos(ang), jnp.sin(ang)], axis=-1)
    feats = jnp.pad(feats, ((0, 0), (0, LANES - HY_EMB)))
    max_decay = math.log(HY_TARGET) / HY_FAST_DECAY
    min_decay = math.log(HY_TARGET) / HY_SLOW_DECAY
    deltas = jnp.abs(jnp.linspace(min_decay, max_decay, BRANCH_W, dtype=F32))[None, :]
    tl = min(512, length)
    ncol = HY_ORDER * 2 * BRANCH_W
    full = lambda r, c: pl.BlockSpec((r, c), lambda i: (0, 0))
    return pl.pallas_call(
        _hy_filter_kernel,
        grid=(length // tl,),
        in_specs=[pl.BlockSpec((tl, LANES), lambda i: (i, 0)),
                  full(LANES, HY_FFN), full(1, HY_FFN), full(HY_FFN, HY_FFN), full(1, HY_FFN),
                  full(HY_FFN, ncol), full(2, HY_FFN), full(1, BRANCH_W)],
        out_specs=pl.BlockSpec((tl, ncol), lambda i: (i, 0)),
        out_shape=jax.ShapeDtypeStruct((length, ncol), F32),
        compiler_params=_cp(("parallel",)),
        name="hyena_filters",
    )(feats, p["hy_w1"], p["hy_b1"], p["hy_w2"], p["hy_b2"], p["hy_w3"], p["hy_freq"], deltas)


def _dft_tables(n1):
    n2 = FFT_N2
    n = n1 * n2
    two_pi = 2.0 * math.pi

    def cs(idx, period):
        ang = (two_pi / period) * (idx % period).astype(F32)
        return jnp.cos(ang), jnp.sin(ang)

    k1 = jnp.arange(n1, dtype=jnp.int32)
    c, s = cs(k1[:, None] * k1[None, :], n1)
    half = n1 // 2
    ch, sh = c[:, :half], s[:, :half]
    m1_re = jnp.concatenate([ch, sh], axis=1)
    m1_im = jnp.concatenate([-sh, ch], axis=1)
    m1 = jnp.stack([m1_re, m1_im], axis=1).reshape(2 * n1, n1)
    m1_taps = jnp.stack([c, -s], axis=1).reshape(2 * n1, n1)
    ci, si = ch.T / n, sh.T / n
    m2_re = jnp.stack([ci, -si], axis=2).reshape(half, 2 * n1)
    m2_im = jnp.stack([si, ci], axis=2).reshape(half, 2 * n1)
    m2 = jnp.concatenate([m2_re, m2_im], axis=0)
    k2 = jnp.arange(n2, dtype=jnp.int32)
    idx = k2[None, None, :] * (k2[None, :, None] * n1 + k1[:, None, None])
    gc, gs = cs(idx, n)
    g_fwd = jnp.concatenate([jnp.concatenate([gc, gs], axis=2),
                             jnp.concatenate([-gs, gc], axis=2)], axis=1)
    gct, gst = jnp.swapaxes(gc, 1, 2), jnp.swapaxes(gs, 1, 2)
    g_inv = jnp.concatenate([jnp.concatenate([gct, -gst], axis=2),
                             jnp.concatenate([gst, gct], axis=2)], axis=1)
    return m1, m1_taps, m2, g_fwd, g_inv


def _stage1_kernel(m_ref, *refs, nin, prec):
    o_ref = refs[-1]
    parts = [r[0] for r in refs[:nin]]
    x = parts[0] if nin == 1 else jnp.concatenate(parts, axis=0)
    o_ref[0] = jnp.dot(m_ref[...], x, precision=prec,
                       preferred_element_type=F32).astype(o_ref.dtype)


def _fft_chunk(ncols):
    return min(8192, ncols)


FFT_SUB = 16
FFT_NSUB = FFT_N2 // FFT_SUB


def _stage1_data_kernel(m_ref, x_ref, o_ref):
    rows = x_ref.shape[0] * x_ref.shape[1] * FFT_SUB
    x = x_ref[...].reshape(rows, x_ref.shape[-1])
    t = jnp.dot(m_ref[...], x, preferred_element_type=F32)
    o_ref[0, 0] = t.astype(o_ref.dtype).reshape(o_ref.shape[2:])


def _stage1_data(zsrc, zi, mbig1):
    nsrc, bsz, lp, w = zsrc.shape
    half = lp // FFT_N2
    npair = bsz // 2
    zv = zsrc.reshape(nsrc, bsz, half, FFT_NSUB, FFT_SUB, w)
    return pl.pallas_call(
        _stage1_data_kernel,
        grid=(npair, FFT_NSUB),
        in_specs=[pl.BlockSpec(mbig1.shape, lambda p, j: (0, 0)),
                  pl.BlockSpec((None, 2, half, None, FFT_SUB, w),
                               lambda p, j: (zi, p, 0, j, 0, 0))],
        out_specs=pl.BlockSpec((1, 1, 4 * half, FFT_SUB, w), lambda p, j: (p, j, 0, 0, 0)),
        out_shape=jax.ShapeDtypeStruct((npair, FFT_NSUB, 4 * half, FFT_SUB, w), BF16),
        compiler_params=_cp(("parallel", "parallel"), 48 * 1024 * 1024),
        name="dft_stage1",
    )(mbig1, zv)


def _stage1_taps(taps, m1t):
    nord, n1, ncols = taps.shape
    ch = _fft_chunk(ncols)
    return pl.pallas_call(
        functools.partial(_stage1_kernel, nin=1, prec=HIGHEST),
        grid=(nord, ncols // ch),
        in_specs=[pl.BlockSpec(m1t.shape, lambda p, j: (0, 0)),
                  pl.BlockSpec((1, n1, ch), lambda p, j: (p, 0, j))],
        out_specs=pl.BlockSpec((1, 2 * n1, ch), lambda p, j: (p, 0, j)),
        out_shape=jax.ShapeDtypeStruct((nord, 2 * n1, ncols), F32),
        compiler_params=_cp(("parallel", "parallel")),
        name="dft_stage1_taps",
    )(m1t, taps)


def _mid_taps_kernel(g_ref, t_ref, o_ref):
    o_ref[0, 0] = jnp.dot(g_ref[0], t_ref[0, 0], precision=HIGHEST, preferred_element_type=F32)


def _mid_taps(t, g_fwd):
    nord, n1, rows, w = t.shape
    return pl.pallas_call(
        _mid_taps_kernel,
        grid=(n1, nord),
        in_specs=[pl.BlockSpec((1, rows, rows), lambda k, o: (k, 0, 0)),
                  pl.BlockSpec((1, 1, rows, w), lambda k, o: (o, k, 0, 0))],
        out_specs=pl.BlockSpec((1, 1, rows, w), lambda k, o: (o, k, 0, 0)),
        out_shape=jax.ShapeDtypeStruct(t.shape, F32),
        compiler_params=_cp(("parallel", "parallel")),
        name="dft_mid_taps",
    )(g_fwd, t)


def _mid_kernel(gf_ref, gi_ref, k_ref, t_ref, o_ref):
    n2 = FFT_N2
    w = t_ref.shape[-1]
    t = jnp.concatenate([t_ref[0, :, 0, 0].reshape(n2, w), t_ref[0, :, 0, 1].reshape(n2, w)], axis=0)
    z = jnp.dot(gf_ref[0], t, preferred_element_type=F32)
    zr, zi = z[:n2], z[n2:]
    kr, ki = k_ref[0, :n2], k_ref[0, n2:]
    y = jnp.concatenate([zr * kr - zi * ki, zr * ki + zi * kr], axis=0).astype(BF16)
    u = jnp.dot(gi_ref[0], y, preferred_element_type=F32).astype(o_ref.dtype)
    o_ref[0, :, 0, 0] = u[:n2].reshape(FFT_NSUB, FFT_SUB, w)
    o_ref[0, :, 0, 1] = u[n2:].reshape(FFT_NSUB, FFT_SUB, w)


def _mid(t, kspec, g_fwd, g_inv):
    npair, nsub, rows, sub, w = t.shape
    n1 = rows // 2
    tv = t.reshape(npair, nsub, n1, 2, sub, w)
    gspec = pl.BlockSpec((1, 2 * FFT_N2, 2 * FFT_N2), lambda k, p: (k, 0, 0))
    tspec = pl.BlockSpec((1, nsub, 1, 2, sub, w), lambda k, p: (p, 0, k, 0, 0, 0))
    out = pl.pallas_call(
        _mid_kernel,
        grid=(n1, npair),
        in_specs=[gspec, gspec,
                  pl.BlockSpec((1, 2 * FFT_N2, w), lambda k, p: (k, 0, 0)),
                  tspec],
        out_specs=tspec,
        out_shape=jax.ShapeDtypeStruct(tv.shape, BF16),
        compiler_params=_cp(("parallel", "arbitrary")),
        name="dft_mid",
    )(g_fwd, g_inv, kspec, tv)
    return out.reshape(t.shape)


def _last_kernel(m_ref, u_ref, z_ref, x_ref, bias_ref, o_ref):
    w = u_ref.shape[-1]
    u = u_ref[0, 0].reshape(u_ref.shape[2] * FFT_SUB, w)
    y = jnp.dot(m_ref[...], u, preferred_element_type=F32)
    rows = y.shape[0]
    z = z_ref[...].reshape(rows, w)
    x = x_ref[...].reshape(rows, w)
    conv = (y + z.astype(F32) * bias_ref[...]).astype(BF16)
    o_ref[...] = (x * conv).astype(o_ref.dtype).reshape(o_ref.shape)


def _last(u, zsrc, zi, xsrc, xi, bias_row, mbig2):
    npair, nsub, rows, sub, w = u.shape
    half = rows // 4
    bsz, lp = 2 * npair, half * FFT_N2
    view = lambda a: a.reshape(a.shape[0], bsz, half, nsub, sub, w)
    tok = lambda k: pl.BlockSpec((None, 2, half, None, sub, w), lambda p, j: (k, p, 0, j, 0, 0))
    out = pl.pallas_call(
        _last_kernel,
        grid=(npair, nsub),
        in_specs=[pl.BlockSpec(mbig2.shape, lambda p, j: (0, 0)),
                  pl.BlockSpec((1, 1, rows, sub, w), lambda p, j: (p, j, 0, 0, 0)),
                  tok(zi), tok(xi),
                  pl.BlockSpec((1, w), lambda p, j: (0, 0))],
        out_specs=tok(0),
        out_shape=jax.ShapeDtypeStruct((1, bsz, half, nsub, sub, w), BF16),
        compiler_params=_cp(("parallel", "parallel"), 48 * 1024 * 1024),
        name="dft_last",
    )(mbig2, u, view(zsrc), view(xsrc), bias_row)
    return out.reshape(1, bsz, lp, w)


def _hyena_spectra(length, p):
    lp = max(length, FFT_MIN_LEN)
    n1 = 2 * lp // FFT_N2
    n = 2 * lp
    w = BRANCH_W
    filt = _hy_filters(length, p).reshape(length, HY_ORDER, 2, w)
    h_fwd = jnp.moveaxis(filt[:, :, 0], 1, 0)
    h_bwd = jnp.moveaxis(filt[:, :, 1], 1, 0)
    taps = jnp.concatenate([h_fwd, jnp.zeros((HY_ORDER, n - 2 * length + 1, w), F32),
                            h_bwd[:, :0:-1]], axis=1)
    m1, m1t, m2, g_fwd, g_inv = _dft_tables(n1)
    t = _stage1_taps(taps.reshape(HY_ORDER, n1, FFT_N2 * w), m1t)
    kspec = _mid_taps(t.reshape(HY_ORDER, n1, 2 * FFT_N2, w), g_fwd)
    eye = jnp.eye(FFT_SUB, dtype=F32)
    return kspec, (jnp.kron(m1, eye).astype(BF16), jnp.kron(m2, eye).astype(BF16),
                   g_fwd.astype(BF16), g_inv.astype(BF16))


def _hyena(proj, p, kspec, tables):
    bsz, length, _ = proj.shape
    mbig1, mbig2, g_fwd, g_inv = tables
    lp = mbig2.shape[0] // (2 * FFT_SUB) * FFT_N2
    uc = _hy_conv(proj, p)
    if lp != length:
        uc = jnp.pad(uc, ((0, 0), (0, 0), (0, lp - length), (0, 0)))
    zsrc, zi = uc, 0
    for order in range(HY_ORDER):
        t = _stage1_data(zsrc, zi, mbig1)
        u = _mid(t, kspec[order], g_fwd, g_inv)
        zsrc, zi = _last(u, zsrc, zi, uc, order + 1, p["hy_bias"][order:order + 1], mbig2), 0
    return zsrc[0, :, :length]


def _merge_kernel(x_ref, ya_ref, yb_ref, yc_ref, g0_ref, g1_ref, g2_ref, wb_ref, wo_ref,
                  sn_ref, n1_ref, gate_ref, o_ref):
    yc = _rms(yc_ref[0].astype(F32), sn_ref[...]).astype(BF16)
    merged = None
    for k, (y, g_ref) in enumerate(((ya_ref[0], g0_ref), (yb_ref[0], g1_ref), (yc, g2_ref))):
        term = jax.nn.sigmoid(g_ref[0].astype(F32)) * jnp.dot(y, wb_ref[k],
                                                              preferred_element_type=F32)
        merged = term if merged is None else merged + term
    out = jnp.dot(merged.astype(BF16), wo_ref[...], preferred_element_type=F32)
    o_ref[0] = x_ref[0] + gate_ref[0] * _rms(out, n1_ref[...])


def _merge(x, proj, ya, yb, yc, p, norm1, mod, row_of, k_gate):
    bsz, length, _ = x.shape
    tm = min(512, length)
    w = BRANCH_W
    tok = lambda c: pl.BlockSpec((1, tm, c), lambda b, i: (b, i, 0))
    gspec = lambda k: pl.BlockSpec((1, tm, D_MODEL),
                                   lambda b, i: (b, i, COL_GATES // D_MODEL + k))
    return pl.pallas_call(
        _merge_kernel,
        grid=(bsz, length // tm),
        in_specs=[tok(D_MODEL), tok(w), tok(w), tok(w), gspec(0), gspec(1), gspec(2),
                  pl.BlockSpec((N_BRANCH, w, D_MODEL), lambda b, i: (0, 0, 0)),
                  pl.BlockSpec((D_MODEL, D_MODEL), lambda b, i: (0, 0)),
                  pl.BlockSpec((1, w), lambda b, i: (0, 0)),
                  pl.BlockSpec((1, D_MODEL), lambda b, i: (0, 0)),
                  pl.BlockSpec((1, 1, D_MODEL), lambda b, i: (row_of(b), 0, k_gate))],
        out_specs=tok(D_MODEL),
        out_shape=jax.ShapeDtypeStruct(x.shape, F32),
        compiler_params=_cp(("parallel", "parallel"), VMEM_BIG),
        name="merge_out",
    )(x, ya, yb, yc, proj, proj, proj, p["w_branch"], p["w_out"], p["ssm_norm"],
      norm1.reshape(1, D_MODEL), mod)


FFN_CH = 256


def _ffn_tail_kernel(x_ref, u_ref, up_ref, un_ref, cw_ref, cb_ref, wd_ref, n3_ref, gate_ref,
                     o_ref, *, grid_w, tm):
    i = pl.program_id(1)
    nt = pl.num_programs(1)
    has_prev = (i > 0).astype(F32)
    has_next = (i < nt - 1).astype(F32)
    rows = tm + 2 * grid_w
    col = lax.broadcasted_iota(jnp.int32, (rows, FFN_CH), 0) % grid_w
    not_first = col != 0
    not_last = col != grid_w - 1

    def conv(c0):
        mid = jnp.concatenate([up_ref[0, :, c0:c0 + FFN_CH] * has_prev.astype(BF16),
                               u_ref[0, :, c0:c0 + FFN_CH],
                               un_ref[0, :, c0:c0 + FFN_CH] * has_next.astype(BF16)], axis=0)
        ext = mid.astype(F32)
        left = jnp.where(not_first, pltpu.roll(ext, 1, 0), 0.0).astype(BF16)
        right = jnp.where(not_last, pltpu.roll(ext, rows - 1, 0), 0.0).astype(BF16)
        acc = cb_ref[:, c0:c0 + FFN_CH].astype(BF16)
        for dr in range(3):
            base = dr * grid_w
            for dc, src in enumerate((left, mid, right)):
                k = dr * 3 + dc
                acc = acc + cw_ref[k:k + 1, c0:c0 + FFN_CH].astype(BF16) * src[base:base + tm]
        return acc

    acc = None
    for j in range(FFN_HIDDEN // FFN_CH):
        val = conv(j * FFN_CH)
        gate = conv(FFN_HIDDEN + j * FFN_CH)
        h = jax.nn.gelu(gate) * val
        part = jnp.dot(h, wd_ref[j * FFN_CH:(j + 1) * FFN_CH, :], preferred_element_type=F32)
        acc = part if acc is None else acc + part
    o_ref[0] = x_ref[0] + gate_ref[0] * _rms(acc, n3_ref[...])


def _ffn_tail(x, u, p, norm3, mod, row_of, k_gate, grid_w):
    bsz, length, _ = x.shape
    tm = min(512, length)
    ncol = 2 * FFN_HIDDEN
    per = tm // grid_w
    nblk = length // grid_w
    return pl.pallas_call(
        functools.partial(_ffn_tail_kernel, grid_w=grid_w, tm=tm),
        grid=(bsz, length // tm),
        in_specs=[pl.BlockSpec((1, tm, D_MODEL), lambda b, i: (b, i, 0)),
                  pl.BlockSpec((1, tm, ncol), lambda b, i: (b, i, 0)),
                  pl.BlockSpec((1, grid_w, ncol), lambda b, i: (b, jnp.maximum(i * per - 1, 0), 0)),
                  pl.BlockSpec((1, grid_w, ncol),
                               lambda b, i: (b, jnp.minimum((i + 1) * per, nblk - 1), 0)),
                  pl.BlockSpec((9, ncol), lambda b, i: (0, 0)),
                  pl.BlockSpec((1, ncol), lambda b, i: (0, 0)),
                  pl.BlockSpec((FFN_HIDDEN, D_MODEL), lambda b, i: (0, 0)),
                  pl.BlockSpec((1, D_MODEL), lambda b, i: (0, 0)),
                  pl.BlockSpec((1, 1, D_MODEL), lambda b, i: (row_of(b), 0, k_gate))],
        out_specs=pl.BlockSpec((1, tm, D_MODEL), lambda b, i: (b, i, 0)),
        out_shape=jax.ShapeDtypeStruct(x.shape, F32),
        compiler_params=_cp(("parallel", "parallel"), VMEM_BIG),
        name="ffn_tail",
    )(x, u, u, u, p["ffn_conv_w"], p["ffn_conv_b"], p["ffn_w_down"],
      norm3.reshape(1, D_MODEL), mod)


def _block_diag(w):
    nb, k, j = w.shape
    eye = jnp.eye(nb, dtype=w.dtype)
    return (eye[:, None, :, None] * w[:, :, None, :]).reshape(nb * k, nb * j)


def _layer_params(l, a):
    ns = SSM_STATE
    w_in = a["w_in"][l]
    xbc0 = BRANCH_W
    b0 = xbc0 + BRANCH_W
    c0 = b0 + SSM_BC
    bc_cols = jnp.concatenate([w_in[:, b0:b0 + ns], w_in[:, c0:c0 + ns],
                               w_in[:, b0 + ns:b0 + 2 * ns], w_in[:, c0 + ns:c0 + 2 * ns]], axis=1)
    out0 = STATE_COLS
    w_main = jnp.concatenate([
        w_in[:, :BRANCH_W],
        w_in[:, xbc0:xbc0 + BRANCH_W],
        bc_cols,
        w_in[:, out0:out0 + BRANCH_W],
        w_in[:, out0 + BRANCH_W:out0 + BRANCH_W + HY_PROJ],
        w_in[:, out0 + BRANCH_W + HY_PROJ:out0 + 2 * BRANCH_W + HY_PROJ],
        w_in[:, out0 + 2 * BRANCH_W + HY_PROJ:],
    ], axis=1).astype(BF16)
    dt0 = BRANCH_W + SSM_XBC

    def per_group(v2):
        rows = []
        for g in range(SSM_GROUPS):
            sel = v2[:, g * HEADS_PER_GROUP:(g + 1) * HEADS_PER_GROUP].reshape(-1)
            rows.append(jnp.pad(sel, (0, LANES - sel.shape[0])))
        return jnp.concatenate(rows)[None, :]

    w_dt_cols = w_in[:, dt0:dt0 + 2 * SSM_HEADS].reshape(D_MODEL, 2, SSM_HEADS)
    w_dt = jnp.concatenate([
        jnp.pad(w_dt_cols[:, :, g * HEADS_PER_GROUP:(g + 1) * HEADS_PER_GROUP].reshape(D_MODEL, -1),
                ((0, 0), (0, LANES - 2 * HEADS_PER_GROUP))) for g in range(SSM_GROUPS)],
        axis=1).astype(BF16)
    scw = a["ssm_conv_w"][l]
    scb = a["ssm_conv_b"][l][None, :]
    perm_bc = lambda v: jnp.concatenate(
        [v[:, BRANCH_W:BRANCH_W + ns], v[:, BRANCH_W + SSM_BC:BRANCH_W + SSM_BC + ns],
         v[:, BRANCH_W + ns:BRANCH_W + 2 * ns], v[:, BRANCH_W + SSM_BC + ns:]], axis=1)
    return {
        "w_main": w_main, "w_dt": w_dt,
        "lru_conv_w": a["lru_conv_w"][l], "lru_conv_b": a["lru_conv_b"][l][None, :],
        "lru_waf": _block_diag(a["lru_w_a"][l, 0]).astype(BF16),
        "lru_wif": _block_diag(a["lru_w_i"][l, 0]).astype(BF16),
        "lru_wab": _block_diag(a["lru_w_a"][l, 1]).astype(BF16),
        "lru_wib": _block_diag(a["lru_w_i"][l, 1]).astype(BF16),
        "lru_bias": jnp.stack([a["lru_b_a"][l, 0], a["lru_b_i"][l, 0],
                               a["lru_b_a"][l, 1], a["lru_b_i"][l, 1]]),
        "lru_lambda": a["lru_lambda"][l],
        "hy_conv_w": a["hy_conv_w"][l], "hy_conv_b": a["hy_conv_b"][l][None, :],
        "hy_w1": jnp.pad(a["hy_w1"][l], ((0, LANES - HY_EMB), (0, 0))),
        "hy_b1": a["hy_b1"][l][None, :], "hy_w2": a["hy_w2"][l], "hy_b2": a["hy_b2"][l][None, :],
        "hy_w3": a["hy_w3"][l], "hy_freq": a["hy_freq"][l], "hy_bias": a["hy_bias"][l],
        "ssm_cw_x": scw[:, :BRANCH_W], "ssm_cb_x": scb[:, :BRANCH_W],
        "ssm_cw_bc": perm_bc(scw), "ssm_cb_bc": perm_bc(scb),
        "ssm_dtb": per_group(a["ssm_dt_bias"][l]), "ssm_alog": per_group(a["ssm_a_log"][l]),
        "ssm_dch": jnp.repeat(a["ssm_d"][l], SSM_HEADDIM)[None, :],
        "ssm_norm": a["ssm_norm"][l][None, :],
        "w_branch": a["w_branch"][l].astype(BF16), "w_out": a["w_out"][l].astype(BF16),
        "ffn_w_up": a["ffn_w_up"][l].astype(BF16),
        "ffn_conv_w": a["ffn_conv_w"][l].reshape(9, 2 * FFN_HIDDEN),
        "ffn_conv_b": a["ffn_conv_b"][l][None, :],
        "ffn_w_down": a["ffn_w_down"][l].astype(BF16),
    }


def _mixer_states(tokens, p, norm0, mod, row_of, base, init):
    proj, dt = _norm_mm(tokens, norm0, mod, row_of, base, base + 1, p["w_main"], p["w_dt"])
    ya, lru_fin = _lru(proj, p, init[0])
    yc, ssm_fin = _ssd(proj, dt, p, init[1])
    return proj, ya, yc, (lru_fin, ssm_fin)


def _layer_tokens(tokens, p, norms_l, mod, row_of, init, hy, grid_w):
    proj, ya, yc, finals = _mixer_states(tokens, p, norms_l[0], mod, row_of, 0, init)
    yb = _hyena(proj, p, *hy)
    tokens = _merge(tokens, proj, ya, yb, yc, p, norms_l[1], mod, row_of, 2)
    u = _norm_mm(tokens, norms_l[2], mod, row_of, 3, 4, p["ffn_w_up"], tn=512)
    tokens = _ffn_tail(tokens, u, p, norms_l[3], mod, row_of, 5, grid_w)
    return tokens, finals


def kernel(x, c, ctx, c_ctx, mod_w, mod_b, norms, w_in, lru_conv_w, lru_conv_b, lru_w_a, lru_b_a,
           lru_w_i, lru_b_i, lru_lambda, hy_conv_w, hy_conv_b, hy_w1, hy_b1, hy_w2, hy_b2, hy_w3,
           hy_freq, hy_bias, ssm_conv_w, ssm_conv_b, ssm_dt_bias, ssm_a_log, ssm_d, ssm_norm,
           w_branch, w_out, ffn_w_up, ffn_conv_w, ffn_conv_b, ffn_w_down):
    a = dict(w_in=w_in, lru_conv_w=lru_conv_w, lru_conv_b=lru_conv_b, lru_w_a=lru_w_a,
             lru_b_a=lru_b_a, lru_w_i=lru_w_i, lru_b_i=lru_b_i, lru_lambda=lru_lambda,
             hy_conv_w=hy_conv_w, hy_conv_b=hy_conv_b, hy_w1=hy_w1, hy_b1=hy_b1, hy_w2=hy_w2,
             hy_b2=hy_b2, hy_w3=hy_w3, hy_freq=hy_freq, hy_bias=hy_bias, ssm_conv_w=ssm_conv_w,
             ssm_conv_b=ssm_conv_b, ssm_dt_bias=ssm_dt_bias, ssm_a_log=ssm_a_log, ssm_d=ssm_d,
             ssm_norm=ssm_norm, w_branch=w_branch, w_out=w_out, ffn_w_up=ffn_w_up,
             ffn_conv_w=ffn_conv_w, ffn_conv_b=ffn_conv_b, ffn_w_down=ffn_w_down)
    bsz, length, _ = x.shape
    ctx_len = ctx.shape[1]
    depth = mod_w.shape[0]
    rows = -(-(bsz + 1) // 8) * 8
    cc = jnp.concatenate([c, c_ctx[None, :], jnp.zeros((rows - bsz - 1, D_MODEL), F32)], axis=0)
    mod_all = _modulation(cc, mod_w, mod_b)
    lat_row = lambda b: b
    ctx_row = lambda b: bsz
    zero_init = (jnp.zeros((bsz, 2, BRANCH_W), F32),
                 jnp.zeros((bsz, 2, SSM_GROUPS, SSM_STATE, GROUP_W), F32))
    for l in range(depth):
        last = l == depth - 1
        p = _layer_params(l, a)
        mod = mod_all[l].reshape(rows, 1, 6 * D_MODEL)
        if last:
            _, _, _, ctx_states = _mixer_states(ctx, p, norms[l, 0], mod, ctx_row, 0, zero_init)
        else:
            hy_ctx = _hyena_spectra(ctx_len, p)
            ctx, ctx_states = _layer_tokens(ctx, p, norms[l], mod, ctx_row, zero_init, hy_ctx,
                                            ctx_len)
        hy_lat = _hyena_spectra(length, p)
        x, _ = _layer_tokens(x, p, norms[l], mod, lat_row, ctx_states, hy_lat, GRID_W)
    return x
```

```python
import functools
import math

import jax
import jax.numpy as jnp
from jax import lax
from jax.experimental import pallas as pl
from jax.experimental.pallas import tpu as pltpu

F32 = jnp.float32
BF16 = jnp.bfloat16
HIGHEST = lax.Precision.HIGHEST

D_MODEL = 1024
DEPTH = 2
GRID_W = 64
BRANCH_W = 512
N_BRANCH = 3
LRU_BLOCKS = 8
LRU_BLOCK_W = BRANCH_W // LRU_BLOCKS
LRU_C = 8.0
HY_ORDER = 2
HY_PROJ = (HY_ORDER + 1) * BRANCH_W
HY_BANDS = 16
HY_EMB = 1 + 2 * HY_BANDS
HY_FFN = 64
HY_TARGET = 1e-2
HY_FAST_DECAY = 0.3
HY_SLOW_DECAY = 1.5
SSM_HEADDIM = 64
SSM_HEADS = BRANCH_W // SSM_HEADDIM
SSM_GROUPS = 2
SSM_STATE = 128
SSM_CHUNK = 128
SSM_BC = SSM_GROUPS * SSM_STATE
SSM_XBC = BRANCH_W + 2 * SSM_BC
FFN_HIDDEN = 2816
RMS_EPS = 1e-6
STATE_COLS = BRANCH_W + SSM_XBC + 2 * SSM_HEADS

COL_XA = 0
COL_XS = 512
COL_BC = 1024
COL_GA = 1536
COL_HY = 2048
COL_Z = 3584
COL_GATES = 4096
MAIN_COLS = 7168
HEADS_PER_GROUP = SSM_HEADS // SSM_GROUPS
GROUP_W = BRANCH_W // SSM_GROUPS
LANES = 128
FFT_N2 = 128
FFT_MIN_LEN = 1024
VMEM_BIG = 56 * 1024 * 1024


def _cp(sem, vmem=None):
    return pltpu.CompilerParams(dimension_semantics=sem, vmem_limit_bytes=vmem)


def _rms(u, g):
    return u * lax.rsqrt(jnp.mean(u * u, axis=-1, keepdims=True) + RMS_EPS) * g


def _softplus(x):
    return jnp.maximum(x, 0.0) + jnp.log1p(jnp.exp(-jnp.abs(x)))


def _mod_kernel(c_ref, w_ref, b_ref, o_ref):
    c = c_ref[...]
    o_ref[0] = jnp.dot(jax.nn.silu(c), w_ref[0], precision=HIGHEST,
                       preferred_element_type=F32) + b_ref[0]


def _modulation(cc, mod_w, mod_b):
    rows = cc.shape[0]
    depth, _, ncol = mod_w.shape
    tn = 1024
    return pl.pallas_call(
        _mod_kernel,
        grid=(depth, ncol // tn),
        in_specs=[pl.BlockSpec((rows, D_MODEL), lambda l, j: (0, 0)),
                  pl.BlockSpec((1, D_MODEL, tn), lambda l, j: (l, 0, j)),
                  pl.BlockSpec((1, 1, tn), lambda l, j: (l, 0, j))],
        out_specs=pl.BlockSpec((1, rows, tn), lambda l, j: (l, 0, j)),
        out_shape=jax.ShapeDtypeStruct((depth, rows, ncol), F32),
        compiler_params=_cp(("parallel", "parallel")),
        name="modulation",
    )(cc, mod_w, mod_b.reshape(depth, 1, ncol))


def _norm_mm_kernel(x_ref, g_ref, sh_ref, sc_ref, w_ref, *rest, has_dt):
    if has_dt:
        wdt_ref, o_ref, odt_ref, hx_ref = rest
    else:
        o_ref, hx_ref = rest

    @pl.when(pl.program_id(2) == 0)
    def _():
        h = _rms(x_ref[0], g_ref[...]) * (1.0 + sc_ref[0]) + sh_ref[0]
        hx_ref[...] = h.astype(BF16)
        if has_dt:
            odt_ref[0] = jnp.dot(hx_ref[...], wdt_ref[...], preferred_element_type=F32)

    o_ref[0] = jnp.dot(hx_ref[...], w_ref[...], preferred_element_type=F32).astype(o_ref.dtype)


def _norm_mm(x, gnorm, mod, row_of, k_shift, k_scale, w, w_dt=None, tn=1024):
    bsz, length, _ = x.shape
    ncol = w.shape[1]
    tm = min(1024, length)
    has_dt = w_dt is not None
    in_specs = [pl.BlockSpec((1, tm, D_MODEL), lambda b, i, j: (b, i, 0)),
                pl.BlockSpec((1, D_MODEL), lambda b, i, j: (0, 0)),
                pl.BlockSpec((1, 1, D_MODEL), lambda b, i, j: (row_of(b), 0, k_shift)),
                pl.BlockSpec((1, 1, D_MODEL), lambda b, i, j: (row_of(b), 0, k_scale)),
                pl.BlockSpec((D_MODEL, tn), lambda b, i, j: (0, j))]
    args = [x, gnorm.reshape(1, D_MODEL), mod, mod, w]
    out_specs = [pl.BlockSpec((1, tm, tn), lambda b, i, j: (b, i, j))]
    out_shape = [jax.ShapeDtypeStruct((bsz, length, ncol), BF16)]
    if has_dt:
        ndt = w_dt.shape[1]
        in_specs.append(pl.BlockSpec((D_MODEL, ndt), lambda b, i, j: (0, 0)))
        args.append(w_dt)
        out_specs.append(pl.BlockSpec((1, tm, ndt), lambda b, i, j: (b, i, 0)))
        out_shape.append(jax.ShapeDtypeStruct((bsz, length, ndt), F32))
    res = pl.pallas_call(
        functools.partial(_norm_mm_kernel, has_dt=has_dt),
        grid=(bsz, length // tm, ncol // tn),
        in_specs=in_specs, out_specs=out_specs, out_shape=out_shape,
        scratch_shapes=[pltpu.VMEM((tm, D_MODEL), BF16)],
        compiler_params=_cp(("parallel", "parallel", "arbitrary"), VMEM_BIG),
        name="norm_proj",
    )(*args)
    return res if has_dt else res[0]


PAD = 8


def _fill_padded(pad_ref, src_ref, length, width, chunk):
    zeros = jnp.zeros((PAD, width), F32)
    pad_ref[pl.ds(0, PAD), :] = zeros
    pad_ref[pl.ds(PAD + length, PAD), :] = zeros

    def body(i, carry):
        t0 = pl.multiple_of(i * chunk, chunk)
        pad_ref[pl.ds(PAD + t0, chunk), :] = src_ref[0, pl.ds(t0, chunk), :].astype(F32)
        return carry

    lax.fori_loop(0, length // chunk, body, 0)


def _conv_rows(pad_ref, t0, rows, w, b, left):
    win = pad_ref[pl.ds(t0, rows + 2 * PAD), :]
    acc = b
    for k in range(w.shape[0]):
        off = PAD + k - left
        acc = acc + w[k:k + 1, :] * win[off:off + rows, :]
    return acc


LRU_CHUNK = 128


SUBLANES = 8


def _chunk_scan(a, b, h, reverse):
    rows, width = a.shape
    groups = rows // SUBLANES
    a3 = a.reshape(groups, SUBLANES, width)
    b3 = b.reshape(groups, SUBLANES, width)
    sub = lax.broadcasted_iota(jnp.int32, a3.shape, 1)
    s = 1
    while s < SUBLANES:
        keep = sub < SUBLANES - s if reverse else sub >= s
        shift = SUBLANES - s if reverse else s
        a_s = jnp.where(keep, pltpu.roll(a3, shift, 1), 1.0)
        b_s = jnp.where(keep, pltpu.roll(b3, shift, 1), 0.0)
        b3 = a3 * b_s + b3
        a3 = a3 * a_s
        s *= 2
    out = [None] * groups
    for i in (range(groups - 1, -1, -1) if reverse else range(groups)):
        hs = a3[i] * h + b3[i]
        out[i] = hs
        h = hs[0:1] if reverse else hs[SUBLANES - 1:SUBLANES]
    return jnp.concatenate(out, axis=0), h


def _lru_kernel(xa_ref, ga_ref, cw_ref, cb_ref, waf_ref, wif_ref, wab_ref, wib_ref,
                bias_ref, lam_ref, h0_ref, y_ref, fin_ref, pad_ref, hf_ref, *, length):
    width = xa_ref.shape[-1]
    tc = LRU_CHUNK
    nchunk = length // tc
    _fill_padded(pad_ref, xa_ref, length, width, tc)
    cw = cw_ref[...]
    cb = cb_ref[...]
    neg_c_sp = -LRU_C * _softplus(-lam_ref[...])

    def gated(t0, wa_ref, wi_ref, d):
        xc = _conv_rows(pad_ref, t0, tc, cw, cb, 2)
        xcb = xc.astype(BF16)
        r = jax.nn.sigmoid(jnp.dot(xcb, wa_ref[...], preferred_element_type=F32)
                           + bias_ref[2 * d:2 * d + 1, :])
        i = jax.nn.sigmoid(jnp.dot(xcb, wi_ref[...], preferred_element_type=F32)
                           + bias_ref[2 * d + 1:2 * d + 2, :])
        log_a = neg_c_sp[d:d + 1, :] * r
        a = jnp.exp(log_a)
        gx = jnp.sqrt(jnp.tanh(-log_a) * (1.0 + a * a)) * (i * xc)
        return a, gx

    def fwd_body(c, h):
        t0 = pl.multiple_of(c * tc, tc)
        a, gx = gated(t0, waf_ref, wif_ref, 0)
        hs, h = _chunk_scan(a, gx, h, False)
        hf_ref[pl.ds(t0, tc), :] = hs
        return h

    h_f = lax.fori_loop(0, nchunk, fwd_body, h0_ref[0, 0:1, :])
    fin_ref[0, 0:1, :] = h_f

    def bwd_body(c, h):
        t0 = pl.multiple_of((nchunk - 1 - c) * tc, tc)
        a, gx = gated(t0, wab_ref, wib_ref, 1)
        hs, h = _chunk_scan(a, gx, h, True)
        ga = ga_ref[0, pl.ds(t0, tc), :].astype(F32)
        y = (hf_ref[pl.ds(t0, tc), :] + hs) * jax.nn.gelu(ga)
        y_ref[0, pl.ds(t0, tc), :] = y.astype(y_ref.dtype)
        return h

    h_b = lax.fori_loop(0, nchunk, bwd_body, h0_ref[0, 1:2, :])
    fin_ref[0, 1:2, :] = h_b


def _lru(proj, p, h0):
    bsz, length, _ = proj.shape
    cw_ = GROUP_W
    nsplit = BRANCH_W // cw_
    blk = lambda off: pl.BlockSpec((1, length, cw_), lambda b, j: (b, 0, off // cw_ + j))
    vec = lambda rows: pl.BlockSpec((rows, cw_), lambda b, j: (0, j))
    wspec = pl.BlockSpec((cw_, cw_), lambda b, j: (j, j))
    st = pl.BlockSpec((1, 2, cw_), lambda b, j: (b, 0, j))
    return pl.pallas_call(
        functools.partial(_lru_kernel, length=length),
        grid=(bsz, nsplit),
        in_specs=[blk(COL_XA), blk(COL_GA), vec(4), vec(1), wspec, wspec, wspec, wspec,
                  vec(4), vec(2), st],
        out_specs=[pl.BlockSpec((1, length, cw_), lambda b, j: (b, 0, j)), st],
        out_shape=[jax.ShapeDtypeStruct((bsz, length, BRANCH_W), BF16),
                   jax.ShapeDtypeStruct((bsz, 2, BRANCH_W), F32)],
        scratch_shapes=[pltpu.VMEM((length + 2 * PAD, cw_), F32),
                        pltpu.VMEM((length, cw_), F32)],
        compiler_params=_cp(("parallel", "parallel"), VMEM_BIG),
        name="rglru",
    )(proj, proj, p["lru_conv_w"], p["lru_conv_b"], p["lru_waf"], p["lru_wif"],
      p["lru_wab"], p["lru_wib"], p["lru_bias"], p["lru_lambda"], h0)


def _ssd_kernel(xs_ref, bc_ref, z_ref, dt_ref, cwx_ref, cbx_ref, cwb_ref, cbb_ref,
                dtb_ref, alog_ref, dch_ref, tri_ref, ex_ref, h0_ref, y_ref, fin_ref,
                padx_ref, padb_ref, xc_ref, yf_ref, st_ref, *, length):
    q = SSM_CHUNK
    nchunk = length // q
    gw = xs_ref.shape[-1]
    nh = HEADS_PER_GROUP
    hp = SSM_HEADDIM
    ns = SSM_STATE
    _fill_padded(padx_ref, xs_ref, length, gw, q)
    _fill_padded(padb_ref, bc_ref, length, 2 * ns, q)
    cwx, cbx, cwb, cbb = cwx_ref[...], cbx_ref[...], cwb_ref[...], cbb_ref[...]

    def conv_body(c, carry):
        t0 = pl.multiple_of(c * q, q)
        xc_ref[pl.ds(t0, q), 0:gw] = jax.nn.silu(_conv_rows(padx_ref, t0, q, cwx, cbx, 2)).astype(BF16)
        xc_ref[pl.ds(t0, q), gw:gw + 2 * ns] = jax.nn.silu(
            _conv_rows(padb_ref, t0, q, cwb, cbb, 2)).astype(BF16)
        return carry

    lax.fori_loop(0, nchunk, conv_body, 0)

    dtb = dtb_ref[...]
    a_neg = -jnp.exp(alog_ref[...])
    rid = lax.broadcasted_iota(jnp.int32, (q, q), 0)
    cid = lax.broadcasted_iota(jnp.int32, (q, q), 1)
    lower = rid >= cid
    upper = rid <= cid
    lane_head = lax.broadcasted_iota(jnp.int32, (q, gw), 1) // hp

    def chunk(t0, d):
        keep, edge = (lower, q - 1) if d == 0 else (upper, 0)
        dt = _softplus(dt_ref[0, pl.ds(t0, q), :] + dtb)
        cs = jnp.dot(tri_ref[d], jnp.concatenate(_split3(dt * a_neg), axis=0),
                     preferred_element_type=F32)
        cs_t = cs.T
        dt_x = jnp.dot(jnp.concatenate(_split3(dt)[:2], axis=1), ex_ref[d, :, 0:gw],
                       preferred_element_type=F32)
        cs_all = jnp.dot(jnp.concatenate(_split3(cs)[:2], axis=1), ex_ref[d],
                         preferred_element_type=F32)
        cs_x = cs_all[:, 0:gw]
        xs = xc_ref[pl.ds(t0, q), 0:gw].astype(F32)
        bm = xc_ref[pl.ds(t0, q), gw:gw + ns]
        cm = xc_ref[pl.ds(t0, q), gw + ns:gw + 2 * ns]
        bm_t = bm.astype(F32).T.astype(BF16)
        cb = jnp.dot(cm, bm_t, preferred_element_type=F32)
        xdt = xs * dt_x
        xdt_b = xdt.astype(BF16)
        mats = []
        for j in range(nh):
            lane = d * nh + j
            seg = cs_all[:, gw + j * q:gw + (j + 1) * q] - cs_t[lane:lane + 1, :]
            mats.append((cb * jnp.exp(jnp.where(keep, seg, -jnp.inf))).astype(BF16))
        y_all = jnp.dot(jnp.concatenate(mats, axis=0), xdt_b, preferred_element_type=F32)
        y_diag = y_all[(nh - 1) * q:nh * q]
        for j in range(nh - 2, -1, -1):
            y_diag = jnp.where(lane_head == j, y_all[j * q:(j + 1) * q], y_diag)
        st = st_ref[d]
        y_off = jnp.dot(cm, st.astype(BF16), preferred_element_type=F32) * jnp.exp(cs_x)
        cs_edge = cs_x[edge:edge + 1, :]
        xw = (xdt * jnp.exp(cs_edge - cs_x)).astype(BF16)
        st_ref[d] = jnp.exp(cs_edge) * st + jnp.dot(bm_t, xw, preferred_element_type=F32)
        return y_diag + y_off, xs

    st_ref[...] = h0_ref[0, :, 0]
    dch = dch_ref[...]

    def finish(t0, ys, xs):
        z = z_ref[0, pl.ds(t0, q), :].astype(F32)
        y_ref[0, pl.ds(t0, q), :] = ((ys + dch * xs) * jax.nn.silu(z)).astype(y_ref.dtype)

    def sweep_pair(c):
        tf = pl.multiple_of(c * q, q)
        tb = pl.multiple_of((nchunk - 1 - c) * q, q)
        yf, xf = chunk(tf, 0)
        yb, xb = chunk(tb, 1)
        return tf, tb, yf, xf, yb, xb

    def park_body(c, carry):
        tf, tb, yf, _, yb, _ = sweep_pair(c)
        yf_ref[pl.ds(tf, q), :] = yf
        yf_ref[pl.ds(tb, q), :] = yb
        return carry

    def finish_body(c, carry):
        tf, tb, yf, xf, yb, xb = sweep_pair(c)
        finish(tf, yf_ref[pl.ds(tf, q), :] + yf, xf)
        finish(tb, yf_ref[pl.ds(tb, q), :] + yb, xb)
        return carry

    lax.fori_loop(0, nchunk // 2, park_body, 0)
    if nchunk % 2:
        t_mid = (nchunk // 2) * q
        yf, xf = chunk(t_mid, 0)
        yb, _ = chunk(t_mid, 1)
        finish(t_mid, yf + yb, xf)
    lax.fori_loop((nchunk + 1) // 2, nchunk, finish_body, 0)
    fin_ref[0, :, 0] = st_ref[...]


def _split3(x):
    hi = x.astype(BF16)
    rest = x - hi.astype(F32)
    mid = rest.astype(BF16)
    return hi, mid, (rest - mid.astype(F32)).astype(BF16)


def _ssd_tables():
    q, nh, hp, gw = SSM_CHUNK, HEADS_PER_GROUP, SSM_HEADDIM, GROUP_W
    r = jnp.arange(q)
    lower = (r[:, None] >= r[None, :]).astype(BF16)
    tri = jnp.stack([jnp.tile(lower, (1, 3)), jnp.tile(lower.T, (1, 3))])
    src = jnp.arange(LANES)[:, None]
    col = jnp.arange(gw + nh * q)[None, :]
    head_of_col = jnp.where(col < gw, col // hp, (col - gw) // q)
    expand = jnp.stack([jnp.tile((src == d * nh + head_of_col).astype(BF16), (2, 1))
                        for d in range(2)])
    return tri, expand


def _ssd(proj, dt, p, h0):
    bsz, length, _ = proj.shape
    gw, ns, nh, hp = GROUP_W, SSM_STATE, HEADS_PER_GROUP, SSM_HEADDIM
    tri, expand = _ssd_tables()
    blk = lambda off: pl.BlockSpec((1, length, gw), lambda b, g: (b, 0, off // gw + g))
    vec = lambda rows, w: pl.BlockSpec((rows, w), lambda b, g: (0, g))
    full = lambda a: pl.BlockSpec(a.shape, lambda b, g: (0,) * a.ndim)
    st = pl.BlockSpec((1, 2, 1, ns, gw), lambda b, g: (b, 0, g, 0, 0))
    return pl.pallas_call(
        functools.partial(_ssd_kernel, length=length),
        grid=(bsz, SSM_GROUPS),
        in_specs=[blk(COL_XS), blk(COL_BC), blk(COL_Z),
                  pl.BlockSpec((1, length, LANES), lambda b, g: (b, 0, g)),
                  vec(4, gw), vec(1, gw), vec(4, 2 * ns), vec(1, 2 * ns),
                  vec(1, LANES), vec(1, LANES), vec(1, gw), full(tri), full(expand), st],
        out_specs=[pl.BlockSpec((1, length, gw), lambda b, g: (b, 0, g)), st],
        out_shape=[jax.ShapeDtypeStruct((bsz, length, BRANCH_W), BF16),
                   jax.ShapeDtypeStruct((bsz, 2, SSM_GROUPS, ns, gw), F32)],
        scratch_shapes=[pltpu.VMEM((length + 2 * PAD, gw), F32),
                        pltpu.VMEM((length + 2 * PAD, 2 * ns), F32),
                        pltpu.VMEM((length, gw + 2 * ns), BF16),
                        pltpu.VMEM((length, gw), F32),
                        pltpu.VMEM((2, ns, gw), F32)],
        compiler_params=_cp(("parallel", "parallel"), VMEM_BIG),
        name="ssd",
    )(proj, proj, proj, dt, p["ssm_cw_x"], p["ssm_cb_x"], p["ssm_cw_bc"], p["ssm_cb_bc"],
      p["ssm_dtb"], p["ssm_alog"], p["ssm_dch"], tri, expand, h0)


HY_CONV_CHUNK = 256


def _hy_conv_kernel(u_ref, w_ref, b_ref, o_ref, pad_ref, *, length):
    width = u_ref.shape[-1]
    tc = min(HY_CONV_CHUNK, length)
    _fill_padded(pad_ref, u_ref, length, width, tc)
    w, b = w_ref[...], b_ref[...]

    def body(c, carry):
        t0 = pl.multiple_of(c * tc, tc)
        o_ref[0, 0, pl.ds(t0, tc), :] = _conv_rows(pad_ref, t0, tc, w, b, 1).astype(o_ref.dtype)
        return carry

    lax.fori_loop(0, length // tc, body, 0)


def _hy_conv(proj, p):
    bsz, length, _ = proj.shape
    w = BRANCH_W
    return pl.pallas_call(
        functools.partial(_hy_conv_kernel, length=length),
        grid=(bsz, 3),
        in_specs=[pl.BlockSpec((1, length, w), lambda b, j: (b, 0, COL_HY // w + j)),
                  pl.BlockSpec((3, w), lambda b, j: (0, j)),
                  pl.BlockSpec((1, w), lambda b, j: (0, j))],
        out_specs=pl.BlockSpec((1, 1, length, w), lambda b, j: (j, b, 0, 0)),
        out_shape=jax.ShapeDtypeStruct((3, bsz, length, w), BF16),
        scratch_shapes=[pltpu.VMEM((length + 2 * PAD, w), F32)],
        compiler_params=_cp(("parallel", "parallel"), VMEM_BIG),
        name="hyena_short_conv",
    )(proj, p["hy_conv_w"], p["hy_conv_b"])


def _hy_filter_kernel(f_ref, w1_ref, b1_ref, w2_ref, b2_ref, w3_ref, fr_ref, dl_ref, o_ref):
    feats = f_ref[...]
    h = jnp.sin(fr_ref[0:1, :] * (jnp.dot(feats, w1_ref[...], precision=HIGHEST,
                                          preferred_element_type=F32) + b1_ref[...]))
    h = jnp.sin(fr_ref[1:2, :] * (jnp.dot(h, w2_ref[...], precision=HIGHEST,
                                          preferred_element_type=F32) + b2_ref[...]))
    h = jnp.dot(h, w3_ref[...], precision=HIGHEST, preferred_element_type=F32)
    window = jnp.exp(-feats[:, 0:1] * dl_ref[...])
    fwd = feats[:, HY_EMB:HY_EMB + 1] * window
    bwd = feats[:, HY_EMB + 1:HY_EMB + 2] * window
    w = BRANCH_W
    for o in range(HY_ORDER):
        o_ref[o] = h[:, 2 * o * w:(2 * o + 1) * w] * fwd + h[:, (2 * o + 1) * w:(2 * o + 2) * w] * bwd


def _hy_taps(length, n, p):
    pos = jnp.arange(length, dtype=F32)[:, None]
    t01 = jnp.linspace(0.0, 1.0, length, dtype=F32)[:, None]
    bands = jnp.linspace(1e-4, HY_BANDS - 1, HY_BANDS, dtype=F32)
    ang = (2.0 * math.pi / length) * pos * bands
    feats = jnp.concatenate([t01, jnp.cos(ang), jnp.sin(ang)], axis=-1)

    def flagged(f, col):
        flags = jnp.zeros((f.shape[0], LANES - HY_EMB), F32).at[:, col].set(1.0)
        return jnp.concatenate([f, flags], axis=1)

    feats = jnp.concatenate([flagged(feats, 0), jnp.zeros((n - 2 * length + 1, LANES), F32),
                             flagged(feats[:0:-1], 1)], axis=0)
    max_decay = math.log(HY_TARGET) / HY_FAST_DECAY
    min_decay = math.log(HY_TARGET) / HY_SLOW_DECAY
    deltas = jnp.abs(jnp.linspace(min_decay, max_decay, BRANCH_W, dtype=F32))[None, :]
    tl = 512
    ncol = HY_ORDER * 2 * BRANCH_W
    full = lambda r, c: pl.BlockSpec((r, c), lambda i: (0, 0))
    return pl.pallas_call(
        _hy_filter_kernel,
        grid=(n // tl,),
        in_specs=[pl.BlockSpec((tl, LANES), lambda i: (i, 0)),
                  full(LANES, HY_FFN), full(1, HY_FFN), full(HY_FFN, HY_FFN), full(1, HY_FFN),
                  full(HY_FFN, ncol), full(2, HY_FFN), full(1, BRANCH_W)],
        out_specs=pl.BlockSpec((HY_ORDER, tl, BRANCH_W), lambda i: (0, i, 0)),
        out_shape=jax.ShapeDtypeStruct((HY_ORDER, n, BRANCH_W), F32),
        compiler_params=_cp(("parallel",)),
        name="hyena_filters",
    )(feats, p["hy_w1"], p["hy_b1"], p["hy_w2"], p["hy_b2"], p["hy_w3"], p["hy_freq"], deltas)


def _dft_tables(n1):
    n2 = FFT_N2
    n = n1 * n2
    two_pi = 2.0 * math.pi

    def cs(idx, period):
        ang = (two_pi / period) * (idx % period).astype(F32)
        return jnp.cos(ang), jnp.sin(ang)

    k1 = jnp.arange(n1, dtype=jnp.int32)
    c, s = cs(k1[:, None] * k1[None, :], n1)
    half = n1 // 2
    ch, sh = c[:, :half], s[:, :half]
    m1_re = jnp.concatenate([ch, sh], axis=1)
    m1_im = jnp.concatenate([-sh, ch], axis=1)
    m1 = jnp.stack([m1_re, m1_im], axis=1).reshape(2 * n1, n1)
    m1_taps = jnp.stack([c, -s], axis=1).reshape(2 * n1, n1)
    ci, si = ch.T / n, sh.T / n
    m2_re = jnp.stack([ci, -si], axis=2).reshape(half, 2 * n1)
    m2_im = jnp.stack([si, ci], axis=2).reshape(half, 2 * n1)
    m2 = jnp.concatenate([m2_re, m2_im], axis=0)
    k2 = jnp.arange(n2, dtype=jnp.int32)
    idx = k2[None, None, :] * (k2[None, :, None] * n1 + k1[:, None, None])
    gc, gs = cs(idx, n)
    g_fwd = jnp.concatenate([jnp.concatenate([gc, gs], axis=2),
                             jnp.concatenate([-gs, gc], axis=2)], axis=1)
    gct, gst = jnp.swapaxes(gc, 1, 2), jnp.swapaxes(gs, 1, 2)
    g_inv = jnp.concatenate([jnp.concatenate([gct, -gst], axis=2),
                             jnp.concatenate([gst, gct], axis=2)], axis=1)
    return m1, m1_taps, m2, g_fwd, g_inv


def _stage1_kernel(m_ref, *refs, nin, prec):
    o_ref = refs[-1]
    parts = [r[0] for r in refs[:nin]]
    x = parts[0] if nin == 1 else jnp.concatenate(parts, axis=0)
    o_ref[0] = jnp.dot(m_ref[...], x, precision=prec,
                       preferred_element_type=F32).astype(o_ref.dtype)


def _fft_chunk(ncols):
    return min(8192, ncols)


FFT_SUB = 16
FFT_NSUB = FFT_N2 // FFT_SUB


def _stage1_data_kernel(m_ref, x_ref, o_ref):
    rows = x_ref.shape[0] * x_ref.shape[1] * FFT_SUB
    x = x_ref[...].reshape(rows, x_ref.shape[-1])
    t = jnp.dot(m_ref[...], x, preferred_element_type=F32)
    o_ref[0, 0] = t.astype(o_ref.dtype).reshape(o_ref.shape[2:])


def _stage1_data(zsrc, zi, mbig1):
    nsrc, bsz, lp, w = zsrc.shape
    half = lp // FFT_N2
    npair = bsz // 2
    zv = zsrc.reshape(nsrc, bsz, half, FFT_NSUB, FFT_SUB, w)
    return pl.pallas_call(
        _stage1_data_kernel,
        grid=(npair, FFT_NSUB),
        in_specs=[pl.BlockSpec(mbig1.shape, lambda p, j: (0, 0)),
                  pl.BlockSpec((None, 2, half, None, FFT_SUB, w),
                               lambda p, j: (zi, p, 0, j, 0, 0))],
        out_specs=pl.BlockSpec((1, 1, 4 * half, FFT_SUB, w), lambda p, j: (p, j, 0, 0, 0)),
        out_shape=jax.ShapeDtypeStruct((npair, FFT_NSUB, 4 * half, FFT_SUB, w), BF16),
        compiler_params=_cp(("parallel", "parallel"), 48 * 1024 * 1024),
        name="dft_stage1",
    )(mbig1, zv)


def _stage1_taps(taps, m1t):
    nord, n1, ncols = taps.shape
    ch = _fft_chunk(ncols)
    return pl.pallas_call(
        functools.partial(_stage1_kernel, nin=1, prec=HIGHEST),
        grid=(nord, ncols // ch),
        in_specs=[pl.BlockSpec(m1t.shape, lambda p, j: (0, 0)),
                  pl.BlockSpec((1, n1, ch), lambda p, j: (p, 0, j))],
        out_specs=pl.BlockSpec((1, 2 * n1, ch), lambda p, j: (p, 0, j)),
        out_shape=jax.ShapeDtypeStruct((nord, 2 * n1, ncols), F32),
        compiler_params=_cp(("parallel", "parallel")),
        name="dft_stage1_taps",
    )(m1t, taps)


def _mid_taps_kernel(g_ref, t_ref, o_ref):
    o_ref[0, 0] = jnp.dot(g_ref[0], t_ref[0, 0], precision=HIGHEST, preferred_element_type=F32)


def _mid_taps(t, g_fwd):
    nord, n1, rows, w = t.shape
    return pl.pallas_call(
        _mid_taps_kernel,
        grid=(n1, nord),
        in_specs=[pl.BlockSpec((1, rows, rows), lambda k, o: (k, 0, 0)),
                  pl.BlockSpec((1, 1, rows, w), lambda k, o: (o, k, 0, 0))],
        out_specs=pl.BlockSpec((1, 1, rows, w), lambda k, o: (o, k, 0, 0)),
        out_shape=jax.ShapeDtypeStruct(t.shape, F32),
        compiler_params=_cp(("parallel", "parallel")),
        name="dft_mid_taps",
    )(g_fwd, t)


def _mid_kernel(gf_ref, gi_ref, k_ref, t_ref, o_ref):
    n2 = FFT_N2
    w = t_ref.shape[-1]
    kr, ki = k_ref[0, :n2], k_ref[0, n2:]
    for p in range(t_ref.shape[0]):
        t = jnp.concatenate([t_ref[p, :, 0, 0].reshape(n2, w),
                             t_ref[p, :, 0, 1].reshape(n2, w)], axis=0)
        z = jnp.dot(gf_ref[0], t, preferred_element_type=F32)
        zr, zi = z[:n2], z[n2:]
        y = jnp.concatenate([zr * kr - zi * ki, zr * ki + zi * kr], axis=0).astype(BF16)
        u = jnp.dot(gi_ref[0], y, preferred_element_type=F32).astype(o_ref.dtype)
        o_ref[p, :, 0, 0] = u[:n2].reshape(FFT_NSUB, FFT_SUB, w)
        o_ref[p, :, 0, 1] = u[n2:].reshape(FFT_NSUB, FFT_SUB, w)


def _mid(t, kspec, g_fwd, g_inv):
    npair, nsub, rows, sub, w = t.shape
    n1 = rows // 2
    tv = t.reshape(npair, nsub, n1, 2, sub, w)
    gspec = pl.BlockSpec((1, 2 * FFT_N2, 2 * FFT_N2), lambda k: (k, 0, 0))
    tspec = pl.BlockSpec((npair, nsub, 1, 2, sub, w), lambda k: (0, 0, k, 0, 0, 0))
    out = pl.pallas_call(
        _mid_kernel,
        grid=(n1,),
        in_specs=[gspec, gspec,
                  pl.BlockSpec((1, 2 * FFT_N2, w), lambda k: (k, 0, 0)),
                  tspec],
        out_specs=tspec,
        out_shape=jax.ShapeDtypeStruct(tv.shape, BF16),
        compiler_params=_cp(("parallel",)),
        name="dft_mid",
    )(g_fwd, g_inv, kspec, tv)
    return out.reshape(t.shape)


def _last_kernel(m_ref, u_ref, z_ref, x_ref, bias_ref, o_ref):
    w = u_ref.shape[-1]
    u = u_ref[0, 0].reshape(u_ref.shape[2] * FFT_SUB, w)
    y = jnp.dot(m_ref[...], u, preferred_element_type=F32)
    rows = y.shape[0]
    z = z_ref[...].reshape(rows, w)
    x = x_ref[...].reshape(rows, w)
    conv = (y + z.astype(F32) * bias_ref[...]).astype(BF16)
    o_ref[...] = (x * conv).astype(o_ref.dtype).reshape(o_ref.shape)


def _last(u, zsrc, zi, xsrc, xi, bias_row, mbig2):
    npair, nsub, rows, sub, w = u.shape
    half = rows // 4
    bsz, lp = 2 * npair, half * FFT_N2
    view = lambda a: a.reshape(a.shape[0], bsz, half, nsub, sub, w)
    tok = lambda k: pl.BlockSpec((None, 2, half, None, sub, w), lambda p, j: (k, p, 0, j, 0, 0))
    out = pl.pallas_call(
        _last_kernel,
        grid=(npair, nsub),
        in_specs=[pl.BlockSpec(mbig2.shape, lambda p, j: (0, 0)),
                  pl.BlockSpec((1, 1, rows, sub, w), lambda p, j: (p, j, 0, 0, 0)),
                  tok(zi), tok(xi),
                  pl.BlockSpec((1, w), lambda p, j: (0, 0))],
        out_specs=tok(0),
        out_shape=jax.ShapeDtypeStruct((1, bsz, half, nsub, sub, w), BF16),
        compiler_params=_cp(("parallel", "parallel"), 48 * 1024 * 1024),
        name="dft_last",
    )(mbig2, u, view(zsrc), view(xsrc), bias_row)
    return out.reshape(1, bsz, lp, w)


def _hyena_spectra(length, p):
    lp = max(length, FFT_MIN_LEN)
    n1 = 2 * lp // FFT_N2
    n = 2 * lp
    w = BRANCH_W
    taps = _hy_taps(length, n, p)
    m1, m1t, m2, g_fwd, g_inv = _dft_tables(n1)
    t = _stage1_taps(taps.reshape(HY_ORDER, n1, FFT_N2 * w), m1t)
    kspec = _mid_taps(t.reshape(HY_ORDER, n1, 2 * FFT_N2, w), g_fwd)
    eye = jnp.eye(FFT_SUB, dtype=F32)
    return kspec, (jnp.kron(m1, eye).astype(BF16), jnp.kron(m2, eye).astype(BF16),
                   g_fwd.astype(BF16), g_inv.astype(BF16))


def _hyena(proj, p, kspec, tables):
    bsz, length, _ = proj.shape
    mbig1, mbig2, g_fwd, g_inv = tables
    lp = mbig2.shape[0] // (2 * FFT_SUB) * FFT_N2
    uc = _hy_conv(proj, p)
    if lp != length:
        uc = jnp.pad(uc, ((0, 0), (0, 0), (0, lp - length), (0, 0)))
    zsrc, zi = uc, 0
    for order in range(HY_ORDER):
        t = _stage1_data(zsrc, zi, mbig1)
        u = _mid(t, kspec[order], g_fwd, g_inv)
        zsrc, zi = _last(u, zsrc, zi, uc, order + 1, p["hy_bias"][order:order + 1], mbig2), 0
    return zsrc[0, :, :length]


def _merge_kernel(x_ref, ya_ref, yb_ref, yc_ref, g0_ref, g1_ref, g2_ref, wb_ref, wo_ref,
                  sn_ref, n1_ref, gate_ref, o_ref):
    yc = _rms(yc_ref[0].astype(F32), sn_ref[...]).astype(BF16)
    merged = None
    for k, (y, g_ref) in enumerate(((ya_ref[0], g0_ref), (yb_ref[0], g1_ref), (yc, g2_ref))):
        term = jax.nn.sigmoid(g_ref[0].astype(F32)) * jnp.dot(y, wb_ref[k],
                                                              preferred_element_type=F32)
        merged = term if merged is None else merged + term
    out = jnp.dot(merged.astype(BF16), wo_ref[...], preferred_element_type=F32)
    o_ref[0] = x_ref[0] + gate_ref[0] * _rms(out, n1_ref[...])


def _merge(x, proj, ya, yb, yc, p, norm1, mod, row_of, k_gate):
    bsz, length, _ = x.shape
    tm = min(512, length)
    w = BRANCH_W
    tok = lambda c: pl.BlockSpec((1, tm, c), lambda b, i: (b, i, 0))
    gspec = lambda k: pl.BlockSpec((1, tm, D_MODEL),
                                   lambda b, i: (b, i, COL_GATES // D_MODEL + k))
    return pl.pallas_call(
        _merge_kernel,
        grid=(bsz, length // tm),
        in_specs=[tok(D_MODEL), tok(w), tok(w), tok(w), gspec(0), gspec(1), gspec(2),
                  pl.BlockSpec((N_BRANCH, w, D_MODEL), lambda b, i: (0, 0, 0)),
                  pl.BlockSpec((D_MODEL, D_MODEL), lambda b, i: (0, 0)),
                  pl.BlockSpec((1, w), lambda b, i: (0, 0)),
                  pl.BlockSpec((1, D_MODEL), lambda b, i: (0, 0)),
                  pl.BlockSpec((1, 1, D_MODEL), lambda b, i: (row_of(b), 0, k_gate))],
        out_specs=tok(D_MODEL),
        out_shape=jax.ShapeDtypeStruct(x.shape, F32),
        compiler_params=_cp(("parallel", "parallel"), VMEM_BIG),
        name="merge_out",
    )(x, ya, yb, yc, proj, proj, proj, p["w_branch"], p["w_out"], p["ssm_norm"],
      norm1.reshape(1, D_MODEL), mod)


FFN_CH = 256


def _ffn_kernel(x_ref, xp_ref, xn_ref, n2_ref, sh_ref, sc_ref, wu_ref, cw_ref, cb_ref, wd_ref,
                n3_ref, gate_ref, o_ref, hx_ref, h_ref, *, grid_w, tm):
    i = pl.program_id(1)
    nt = pl.num_programs(1)
    rows = tm + 2 * grid_w
    g2, sc, sh = n2_ref[...], sc_ref[0], sh_ref[0]

    def modulated(x):
        return _rms(x, g2) * (1.0 + sc) + sh

    hx_ref[0:grid_w] = (modulated(xp_ref[0]) * (i > 0).astype(F32)).astype(BF16)
    hx_ref[grid_w:grid_w + tm] = modulated(x_ref[0]).astype(BF16)
    hx_ref[grid_w + tm:rows] = (modulated(xn_ref[0]) * (i < nt - 1).astype(F32)).astype(BF16)

    col = lax.broadcasted_iota(jnp.int32, (rows, FFN_CH), 0) % grid_w
    not_first = col != 0
    not_last = col != grid_w - 1

    def conv(c0):
        ext = jnp.dot(hx_ref[...], wu_ref[:, c0:c0 + FFN_CH], preferred_element_type=F32)
        left = jnp.where(not_first, pltpu.roll(ext, 1, 0), 0.0).astype(BF16)
        right = jnp.where(not_last, pltpu.roll(ext, rows - 1, 0), 0.0).astype(BF16)
        mid = ext.astype(BF16)
        acc = cb_ref[:, c0:c0 + FFN_CH].astype(BF16)
        for dr in range(3):
            base = dr * grid_w
            for dc, src in enumerate((left, mid, right)):
                k = dr * 3 + dc
                acc = acc + cw_ref[k:k + 1, c0:c0 + FFN_CH].astype(BF16) * src[base:base + tm]
        return acc

    for j in range(FFN_HIDDEN // FFN_CH):
        val = conv(j * FFN_CH)
        gate = conv(FFN_HIDDEN + j * FFN_CH)
        h_ref[:, j * FFN_CH:(j + 1) * FFN_CH] = jax.nn.gelu(gate) * val
    down = jnp.dot(h_ref[...], wd_ref[...], preferred_element_type=F32)
    o_ref[0] = x_ref[0] + gate_ref[0] * _rms(down, n3_ref[...])


def _ffn(x, p, norm2, norm3, mod, row_of, k_shift, grid_w):
    bsz, length, _ = x.shape
    tm = min(512, length)
    ncol = 2 * FFN_HIDDEN
    per = tm // grid_w
    nblk = length // grid_w
    const = lambda shape: pl.BlockSpec(shape, lambda b, i: (0,) * len(shape),
                                       pipeline_mode=pl.Buffered(1))
    modv = lambda k: pl.BlockSpec((1, 1, D_MODEL), lambda b, i: (row_of(b), 0, k))
    return pl.pallas_call(
        functools.partial(_ffn_kernel, grid_w=grid_w, tm=tm),
        grid=(bsz, length // tm),
        in_specs=[pl.BlockSpec((1, tm, D_MODEL), lambda b, i: (b, i, 0)),
                  pl.BlockSpec((1, grid_w, D_MODEL),
                               lambda b, i: (b, jnp.maximum(i * per - 1, 0), 0)),
                  pl.BlockSpec((1, grid_w, D_MODEL),
                               lambda b, i: (b, jnp.minimum((i + 1) * per, nblk - 1), 0)),
                  const((1, D_MODEL)), modv(k_shift), modv(k_shift + 1),
                  const((D_MODEL, ncol)), const((9, ncol)), const((1, ncol)),
                  const((FFN_HIDDEN, D_MODEL)), const((1, D_MODEL)), modv(k_shift + 2)],
        out_specs=pl.BlockSpec((1, tm, D_MODEL), lambda b, i: (b, i, 0)),
        out_shape=jax.ShapeDtypeStruct(x.shape, F32),
        scratch_shapes=[pltpu.VMEM((tm + 2 * grid_w, D_MODEL), BF16),
                        pltpu.VMEM((tm, FFN_HIDDEN), BF16)],
        compiler_params=_cp(("parallel", "parallel"), VMEM_BIG),
        name="ffn",
    )(x, x, x, norm2.reshape(1, D_MODEL), mod, mod, p["ffn_w_up"], p["ffn_conv_w"],
      p["ffn_conv_b"], p["ffn_w_down"], norm3.reshape(1, D_MODEL), mod)


def _block_diag(w):
    nb, k, j = w.shape
    eye = jnp.eye(nb, dtype=w.dtype)
    return (eye[:, None, :, None] * w[:, :, None, :]).reshape(nb * k, nb * j)


def _layer_params(l, a):
    ns = SSM_STATE
    w_in = a["w_in"][l]
    xbc0 = BRANCH_W
    b0 = xbc0 + BRANCH_W
    c0 = b0 + SSM_BC
    bc_cols = jnp.concatenate([w_in[:, b0:b0 + ns], w_in[:, c0:c0 + ns],
                               w_in[:, b0 + ns:b0 + 2 * ns], w_in[:, c0 + ns:c0 + 2 * ns]], axis=1)
    out0 = STATE_COLS
    w_main = jnp.concatenate([
        w_in[:, :BRANCH_W],
        w_in[:, xbc0:xbc0 + BRANCH_W],
        bc_cols,
        w_in[:, out0:out0 + BRANCH_W],
        w_in[:, out0 + BRANCH_W:out0 + BRANCH_W + HY_PROJ],
        w_in[:, out0 + BRANCH_W + HY_PROJ:out0 + 2 * BRANCH_W + HY_PROJ],
        w_in[:, out0 + 2 * BRANCH_W + HY_PROJ:],
    ], axis=1).astype(BF16)
    dt0 = BRANCH_W + SSM_XBC

    def per_group(v2):
        rows = []
        for g in range(SSM_GROUPS):
            sel = v2[:, g * HEADS_PER_GROUP:(g + 1) * HEADS_PER_GROUP].reshape(-1)
            rows.append(jnp.pad(sel, (0, LANES - sel.shape[0])))
        return jnp.concatenate(rows)[None, :]

    w_dt_cols = w_in[:, dt0:dt0 + 2 * SSM_HEADS].reshape(D_MODEL, 2, SSM_HEADS)
    w_dt = jnp.concatenate([
        jnp.pad(w_dt_cols[:, :, g * HEADS_PER_GROUP:(g + 1) * HEADS_PER_GROUP].reshape(D_MODEL, -1),
                ((0, 0), (0, LANES - 2 * HEADS_PER_GROUP))) for g in range(SSM_GROUPS)],
        axis=1).astype(BF16)
    scw = a["ssm_conv_w"][l]
    scb = a["ssm_conv_b"][l][None, :]
    perm_bc = lambda v: jnp.concatenate(
        [v[:, BRANCH_W:BRANCH_W + ns], v[:, BRANCH_W + SSM_BC:BRANCH_W + SSM_BC + ns],
         v[:, BRANCH_W + ns:BRANCH_W + 2 * ns], v[:, BRANCH_W + SSM_BC + ns:]], axis=1)
    return {
        "w_main": w_main, "w_dt": w_dt,
        "lru_conv_w": a["lru_conv_w"][l], "lru_conv_b": a["lru_conv_b"][l][None, :],
        "lru_waf": _block_diag(a["lru_w_a"][l, 0]).astype(BF16),
        "lru_wif": _block_diag(a["lru_w_i"][l, 0]).astype(BF16),
        "lru_wab": _block_diag(a["lru_w_a"][l, 1]).astype(BF16),
        "lru_wib": _block_diag(a["lru_w_i"][l, 1]).astype(BF16),
        "lru_bias": jnp.stack([a["lru_b_a"][l, 0], a["lru_b_i"][l, 0],
                               a["lru_b_a"][l, 1], a["lru_b_i"][l, 1]]),
        "lru_lambda": a["lru_lambda"][l],
        "hy_conv_w": a["hy_conv_w"][l], "hy_conv_b": a["hy_conv_b"][l][None, :],
        "hy_w1": jnp.pad(a["hy_w1"][l], ((0, LANES - HY_EMB), (0, 0))),
        "hy_b1": a["hy_b1"][l][None, :], "hy_w2": a["hy_w2"][l], "hy_b2": a["hy_b2"][l][None, :],
        "hy_w3": a["hy_w3"][l], "hy_freq": a["hy_freq"][l], "hy_bias": a["hy_bias"][l],
        "ssm_cw_x": scw[:, :BRANCH_W], "ssm_cb_x": scb[:, :BRANCH_W],
        "ssm_cw_bc": perm_bc(scw), "ssm_cb_bc": perm_bc(scb),
        "ssm_dtb": per_group(a["ssm_dt_bias"][l]), "ssm_alog": per_group(a["ssm_a_log"][l]),
        "ssm_dch": jnp.repeat(a["ssm_d"][l], SSM_HEADDIM)[None, :],
        "ssm_norm": a["ssm_norm"][l][None, :],
        "w_branch": a["w_branch"][l].astype(BF16), "w_out": a["w_out"][l].astype(BF16),
        "ffn_w_up": a["ffn_w_up"][l].astype(BF16),
        "ffn_conv_w": a["ffn_conv_w"][l].reshape(9, 2 * FFN_HIDDEN),
        "ffn_conv_b": a["ffn_conv_b"][l][None, :],
        "ffn_w_down": a["ffn_w_down"][l].astype(BF16),
    }


def _mixer_states(tokens, p, norm0, mod, row_of, base, init):
    proj, dt = _norm_mm(tokens, norm0, mod, row_of, base, base + 1, p["w_main"], p["w_dt"])
    ya, lru_fin = _lru(proj, p, init[0])
    yc, ssm_fin = _ssd(proj, dt, p, init[1])
    return proj, ya, yc, (lru_fin, ssm_fin)


def _layer_tokens(tokens, p, norms_l, mod, row_of, init, hy, grid_w):
    proj, ya, yc, finals = _mixer_states(tokens, p, norms_l[0], mod, row_of, 0, init)
    yb = _hyena(proj, p, *hy)
    tokens = _merge(tokens, proj, ya, yb, yc, p, norms_l[1], mod, row_of, 2)
    tokens = _ffn(tokens, p, norms_l[2], norms_l[3], mod, row_of, 3, grid_w)
    return tokens, finals


def kernel(x, c, ctx, c_ctx, mod_w, mod_b, norms, w_in, lru_conv_w, lru_conv_b, lru_w_a, lru_b_a,
           lru_w_i, lru_b_i, lru_lambda, hy_conv_w, hy_conv_b, hy_w1, hy_b1, hy_w2, hy_b2, hy_w3,
           hy_freq, hy_bias, ssm_conv_w, ssm_conv_b, ssm_dt_bias, ssm_a_log, ssm_d, ssm_norm,
           w_branch, w_out, ffn_w_up, ffn_conv_w, ffn_conv_b, ffn_w_down):
    a = dict(w_in=w_in, lru_conv_w=lru_conv_w, lru_conv_b=lru_conv_b, lru_w_a=lru_w_a,
             lru_b_a=lru_b_a, lru_w_i=lru_w_i, lru_b_i=lru_b_i, lru_lambda=lru_lambda,
             hy_conv_w=hy_conv_w, hy_conv_b=hy_conv_b, hy_w1=hy_w1, hy_b1=hy_b1, hy_w2=hy_w2,
             hy_b2=hy_b2, hy_w3=hy_w3, hy_freq=hy_freq, hy_bias=hy_bias, ssm_conv_w=ssm_conv_w,
             ssm_conv_b=ssm_conv_b, ssm_dt_bias=ssm_dt_bias, ssm_a_log=ssm_a_log, ssm_d=ssm_d,
             ssm_norm=ssm_norm, w_branch=w_branch, w_out=w_out, ffn_w_up=ffn_w_up,
             ffn_conv_w=ffn_conv_w, ffn_conv_b=ffn_conv_b, ffn_w_down=ffn_w_down)
    bsz, length, _ = x.shape
    ctx_len = ctx.shape[1]
    depth = mod_w.shape[0]
    rows = -(-(bsz + 1) // 8) * 8
    cc = jnp.concatenate([c, c_ctx[None, :], jnp.zeros((rows - bsz - 1, D_MODEL), F32)], axis=0)
    mod_all = _modulation(cc, mod_w, mod_b)
    lat_row = lambda b: b
    ctx_row = lambda b: bsz
    zero_init = (jnp.zeros((bsz, 2, BRANCH_W), F32),
                 jnp.zeros((bsz, 2, SSM_GROUPS, SSM_STATE, GROUP_W), F32))
    for l in range(depth):
        last = l == depth - 1
        p = _layer_params(l, a)
        mod = mod_all[l].reshape(rows, 1, 6 * D_MODEL)
        if last:
            _, _, _, ctx_states = _mixer_states(ctx, p, norms[l, 0], mod, ctx_row, 0, zero_init)
        else:
            hy_ctx = _hyena_spectra(ctx_len, p)
            ctx, ctx_states = _layer_tokens(ctx, p, norms[l], mod, ctx_row, zero_init, hy_ctx,
                                            ctx_len)
        hy_lat = _hyena_spectra(length, p)
        x, _ = _layer_tokens(x, p, norms[l], mod, lat_row, ctx_states, hy_lat, GRID_W)
    return x
```

```python
import functools
import math

import jax
import jax.numpy as jnp
from jax import lax
from jax.experimental import pallas as pl
from jax.experimental.pallas import tpu as pltpu

F32 = jnp.float32
BF16 = jnp.bfloat16
HIGHEST = lax.Precision.HIGHEST

D_MODEL = 1024
DEPTH = 2
GRID_W = 64
BRANCH_W = 512
N_BRANCH = 3
LRU_BLOCKS = 8
LRU_BLOCK_W = BRANCH_W // LRU_BLOCKS
LRU_C = 8.0
HY_ORDER = 2
HY_PROJ = (HY_ORDER + 1) * BRANCH_W
HY_BANDS = 16
HY_EMB = 1 + 2 * HY_BANDS
HY_FFN = 64
HY_TARGET = 1e-2
HY_FAST_DECAY = 0.3
HY_SLOW_DECAY = 1.5
SSM_HEADDIM = 64
SSM_HEADS = BRANCH_W // SSM_HEADDIM
SSM_GROUPS = 2
SSM_STATE = 128
SSM_CHUNK = 128
SSM_BC = SSM_GROUPS * SSM_STATE
SSM_XBC = BRANCH_W + 2 * SSM_BC
FFN_HIDDEN = 2816
RMS_EPS = 1e-6
STATE_COLS = BRANCH_W + SSM_XBC + 2 * SSM_HEADS

COL_XA = 0
COL_XS = 512
COL_BC = 1024
COL_GA = 1536
COL_HY = 2048
COL_Z = 3584
COL_GATES = 4096
MAIN_COLS = 7168
HEADS_PER_GROUP = SSM_HEADS // SSM_GROUPS
GROUP_W = BRANCH_W // SSM_GROUPS
LANES = 128
FFT_N2 = 128
FFT_MIN_LEN = 1024
VMEM_BIG = 56 * 1024 * 1024


def _cp(sem, vmem=None):
    return pltpu.CompilerParams(dimension_semantics=sem, vmem_limit_bytes=vmem)


def _rms(u, g):
    return u * lax.rsqrt(jnp.mean(u * u, axis=-1, keepdims=True) + RMS_EPS) * g


def _softplus(x):
    return jnp.maximum(x, 0.0) + jnp.log1p(jnp.exp(-jnp.abs(x)))


def _mod_kernel(c_ref, w_ref, b_ref, o_ref):
    c = c_ref[...]
    o_ref[0] = jnp.dot(jax.nn.silu(c), w_ref[0], precision=HIGHEST,
                       preferred_element_type=F32) + b_ref[0]


def _modulation(cc, mod_w, mod_b):
    rows = cc.shape[0]
    depth, _, ncol = mod_w.shape
    tn = 1024
    return pl.pallas_call(
        _mod_kernel,
        grid=(depth, ncol // tn),
        in_specs=[pl.BlockSpec((rows, D_MODEL), lambda l, j: (0, 0)),
                  pl.BlockSpec((1, D_MODEL, tn), lambda l, j: (l, 0, j)),
                  pl.BlockSpec((1, 1, tn), lambda l, j: (l, 0, j))],
        out_specs=pl.BlockSpec((1, rows, tn), lambda l, j: (l, 0, j)),
        out_shape=jax.ShapeDtypeStruct((depth, rows, ncol), F32),
        compiler_params=_cp(("parallel", "parallel")),
        name="modulation",
    )(cc, mod_w, mod_b.reshape(depth, 1, ncol))


def _norm_mm_kernel(x_ref, g_ref, sh_ref, sc_ref, w_ref, *rest, has_dt):
    if has_dt:
        wdt_ref, o_ref, odt_ref, hx_ref = rest
    else:
        o_ref, hx_ref = rest

    @pl.when(pl.program_id(2) == 0)
    def _():
        h = _rms(x_ref[0], g_ref[...]) * (1.0 + sc_ref[0]) + sh_ref[0]
        hx_ref[...] = h.astype(BF16)
        if has_dt:
            odt_ref[0] = jnp.dot(hx_ref[...], wdt_ref[...], preferred_element_type=F32)

    o_ref[0] = jnp.dot(hx_ref[...], w_ref[...], preferred_element_type=F32).astype(o_ref.dtype)


def _norm_mm(x, gnorm, mod, row_of, k_shift, k_scale, w, w_dt=None, tn=1024):
    bsz, length, _ = x.shape
    ncol = w.shape[1]
    tm = min(1024, length)
    has_dt = w_dt is not None
    in_specs = [pl.BlockSpec((1, tm, D_MODEL), lambda b, i, j: (b, i, 0)),
                pl.BlockSpec((1, D_MODEL), lambda b, i, j: (0, 0)),
                pl.BlockSpec((1, 1, D_MODEL), lambda b, i, j: (row_of(b), 0, k_shift)),
                pl.BlockSpec((1, 1, D_MODEL), lambda b, i, j: (row_of(b), 0, k_scale)),
                pl.BlockSpec((D_MODEL, tn), lambda b, i, j: (0, j))]
    args = [x, gnorm.reshape(1, D_MODEL), mod, mod, w]
    out_specs = [pl.BlockSpec((1, tm, tn), lambda b, i, j: (b, i, j))]
    out_shape = [jax.ShapeDtypeStruct((bsz, length, ncol), BF16)]
    if has_dt:
        ndt = w_dt.shape[1]
        in_specs.append(pl.BlockSpec((D_MODEL, ndt), lambda b, i, j: (0, 0)))
        args.append(w_dt)
        out_specs.append(pl.BlockSpec((1, tm, ndt), lambda b, i, j: (b, i, 0)))
        out_shape.append(jax.ShapeDtypeStruct((bsz, length, ndt), F32))
    res = pl.pallas_call(
        functools.partial(_norm_mm_kernel, has_dt=has_dt),
        grid=(bsz, length // tm, ncol // tn),
        in_specs=in_specs, out_specs=out_specs, out_shape=out_shape,
        scratch_shapes=[pltpu.VMEM((tm, D_MODEL), BF16)],
        compiler_params=_cp(("parallel", "parallel", "arbitrary"), VMEM_BIG),
        name="norm_proj",
    )(*args)
    return res if has_dt else res[0]


PAD = 8


def _fill_padded(pad_ref, src_ref, length, width, chunk):
    zeros = jnp.zeros((PAD, width), F32)
    pad_ref[pl.ds(0, PAD), :] = zeros
    pad_ref[pl.ds(PAD + length, PAD), :] = zeros

    def body(i, carry):
        t0 = pl.multiple_of(i * chunk, chunk)
        pad_ref[pl.ds(PAD + t0, chunk), :] = src_ref[0, pl.ds(t0, chunk), :].astype(F32)
        return carry

    lax.fori_loop(0, length // chunk, body, 0)


def _conv_rows(pad_ref, t0, rows, w, b, left):
    win = pad_ref[pl.ds(t0, rows + 2 * PAD), :]
    acc = b
    for k in range(w.shape[0]):
        off = PAD + k - left
        acc = acc + w[k:k + 1, :] * win[off:off + rows, :]
    return acc


LRU_CHUNK = 128


SUBLANES = 8


def _chunk_scan(a, b, h, reverse):
    rows, width = a.shape
    groups = rows // SUBLANES
    a3 = a.reshape(groups, SUBLANES, width)
    b3 = b.reshape(groups, SUBLANES, width)
    sub = lax.broadcasted_iota(jnp.int32, a3.shape, 1)
    s = 1
    while s < SUBLANES:
        keep = sub < SUBLANES - s if reverse else sub >= s
        shift = SUBLANES - s if reverse else s
        a_s = jnp.where(keep, pltpu.roll(a3, shift, 1), 1.0)
        b_s = jnp.where(keep, pltpu.roll(b3, shift, 1), 0.0)
        b3 = a3 * b_s + b3
        a3 = a3 * a_s
        s *= 2
    out = [None] * groups
    for i in (range(groups - 1, -1, -1) if reverse else range(groups)):
        hs = a3[i] * h + b3[i]
        out[i] = hs
        h = hs[0:1] if reverse else hs[SUBLANES - 1:SUBLANES]
    return jnp.concatenate(out, axis=0), h


def _lru_kernel(xa_ref, ga_ref, cw_ref, cb_ref, waf_ref, wif_ref, wab_ref, wib_ref,
                bias_ref, lam_ref, h0_ref, y_ref, fin_ref, pad_ref, hf_ref, *, length):
    width = xa_ref.shape[-1]
    tc = LRU_CHUNK
    nchunk = length // tc
    _fill_padded(pad_ref, xa_ref, length, width, tc)
    cw = cw_ref[...]
    cb = cb_ref[...]
    neg_c_sp = -LRU_C * _softplus(-lam_ref[...])

    def gated(t0, wa_ref, wi_ref, d):
        xc = _conv_rows(pad_ref, t0, tc, cw, cb, 2)
        xcb = xc.astype(BF16)
        r = jax.nn.sigmoid(jnp.dot(xcb, wa_ref[...], preferred_element_type=F32)
                           + bias_ref[2 * d:2 * d + 1, :])
        i = jax.nn.sigmoid(jnp.dot(xcb, wi_ref[...], preferred_element_type=F32)
                           + bias_ref[2 * d + 1:2 * d + 2, :])
        log_a = neg_c_sp[d:d + 1, :] * r
        a = jnp.exp(log_a)
        gx = jnp.sqrt(jnp.tanh(-log_a) * (1.0 + a * a)) * (i * xc)
        return a, gx

    def fwd_body(c, h):
        t0 = pl.multiple_of(c * tc, tc)
        a, gx = gated(t0, waf_ref, wif_ref, 0)
        hs, h = _chunk_scan(a, gx, h, False)
        hf_ref[pl.ds(t0, tc), :] = hs
        return h

    h_f = lax.fori_loop(0, nchunk, fwd_body, h0_ref[0, 0:1, :])
    fin_ref[0, 0:1, :] = h_f

    def bwd_body(c, h):
        t0 = pl.multiple_of((nchunk - 1 - c) * tc, tc)
        a, gx = gated(t0, wab_ref, wib_ref, 1)
        hs, h = _chunk_scan(a, gx, h, True)
        ga = ga_ref[0, pl.ds(t0, tc), :].astype(F32)
        y = (hf_ref[pl.ds(t0, tc), :] + hs) * jax.nn.gelu(ga)
        y_ref[0, pl.ds(t0, tc), :] = y.astype(y_ref.dtype)
        return h

    h_b = lax.fori_loop(0, nchunk, bwd_body, h0_ref[0, 1:2, :])
    fin_ref[0, 1:2, :] = h_b


def _lru(proj, p, h0):
    bsz, length, _ = proj.shape
    cw_ = GROUP_W
    nsplit = BRANCH_W // cw_
    blk = lambda off: pl.BlockSpec((1, length, cw_), lambda b, j: (b, 0, off // cw_ + j))
    vec = lambda rows: pl.BlockSpec((rows, cw_), lambda b, j: (0, j))
    wspec = pl.BlockSpec((cw_, cw_), lambda b, j: (j, j))
    st = pl.BlockSpec((1, 2, cw_), lambda b, j: (b, 0, j))
    return pl.pallas_call(
        functools.partial(_lru_kernel, length=length),
        grid=(bsz, nsplit),
        in_specs=[blk(COL_XA), blk(COL_GA), vec(4), vec(1), wspec, wspec, wspec, wspec,
                  vec(4), vec(2), st],
        out_specs=[pl.BlockSpec((1, length, cw_), lambda b, j: (b, 0, j)), st],
        out_shape=[jax.ShapeDtypeStruct((bsz, length, BRANCH_W), BF16),
                   jax.ShapeDtypeStruct((bsz, 2, BRANCH_W), F32)],
        scratch_shapes=[pltpu.VMEM((length + 2 * PAD, cw_), F32),
                        pltpu.VMEM((length, cw_), F32)],
        compiler_params=_cp(("parallel", "parallel"), VMEM_BIG),
        name="rglru",
    )(proj, proj, p["lru_conv_w"], p["lru_conv_b"], p["lru_waf"], p["lru_wif"],
      p["lru_wab"], p["lru_wib"], p["lru_bias"], p["lru_lambda"], h0)


def _ssd_kernel(xs_ref, bc_ref, z_ref, dt_ref, cwx_ref, cbx_ref, cwb_ref, cbb_ref,
                dtb_ref, alog_ref, dch_ref, tri_ref, ex_ref, h0_ref, y_ref, fin_ref,
                padx_ref, padb_ref, xc_ref, yd_ref, st_ref, m_ref, xdt_ref, csx_ref, bmt_ref,
                *, length):
    q = SSM_CHUNK
    nchunk = length // q
    gw = xs_ref.shape[-1]
    nh = HEADS_PER_GROUP
    hp = SSM_HEADDIM
    ns = SSM_STATE
    _fill_padded(padx_ref, xs_ref, length, gw, q)
    _fill_padded(padb_ref, bc_ref, length, 2 * ns, q)
    cwx, cbx, cwb, cbb = cwx_ref[...], cbx_ref[...], cwb_ref[...], cbb_ref[...]

    def conv_body(c, carry):
        t0 = pl.multiple_of(c * q, q)
        xc_ref[pl.ds(t0, q), 0:gw] = jax.nn.silu(_conv_rows(padx_ref, t0, q, cwx, cbx, 2)).astype(BF16)
        xc_ref[pl.ds(t0, q), gw:gw + 2 * ns] = jax.nn.silu(
            _conv_rows(padb_ref, t0, q, cwb, cbb, 2)).astype(BF16)
        return carry

    lax.fori_loop(0, nchunk, conv_body, 0)

    dtb = dtb_ref[...]
    a_neg = -jnp.exp(alog_ref[...])
    rid = lax.broadcasted_iota(jnp.int32, (q, q), 0)
    cid = lax.broadcasted_iota(jnp.int32, (q, q), 1)
    lower = rid >= cid
    upper = rid <= cid
    lane_head = lax.broadcasted_iota(jnp.int32, (q, gw), 1) // hp

    def prepare(t0, d, slot):
        keep = lower if d == 0 else upper
        dt = _softplus(dt_ref[0, pl.ds(t0, q), :] + dtb)
        cs = jnp.dot(tri_ref[d], jnp.concatenate(_split3(dt * a_neg), axis=0),
                     preferred_element_type=F32)
        cs_t = cs.T
        dt_x = jnp.dot(jnp.concatenate(_split3(dt)[:2], axis=1), ex_ref[d, :, 0:gw],
                       preferred_element_type=F32)
        cs_all = jnp.dot(jnp.concatenate(_split3(cs)[:2], axis=1), ex_ref[d],
                         preferred_element_type=F32)
        cs_x = cs_all[:, 0:gw]
        xs = xc_ref[pl.ds(t0, q), 0:gw].astype(F32)
        bm = xc_ref[pl.ds(t0, q), gw:gw + ns]
        cm = xc_ref[pl.ds(t0, q), gw + ns:gw + 2 * ns]
        bm_t = bm.astype(F32).T.astype(BF16)
        cb = jnp.dot(cm, bm_t, preferred_element_type=F32)
        xdt_ref[slot, d] = xs * dt_x
        csx_ref[slot, d] = cs_x
        bmt_ref[slot, d] = bm_t
        for j in range(nh):
            lane = d * nh + j
            seg = cs_all[:, gw + j * q:gw + (j + 1) * q] - cs_t[lane:lane + 1, :]
            m_ref[slot, d, j * q:(j + 1) * q, :] = (
                cb * jnp.exp(jnp.where(keep, seg, -jnp.inf))).astype(BF16)

    def consume(t0, d, slot):
        edge = q - 1 if d == 0 else 0
        xdt = xdt_ref[slot, d]
        cs_x = csx_ref[slot, d]
        cm = xc_ref[pl.ds(t0, q), gw + ns:gw + 2 * ns]
        y_all = jnp.dot(m_ref[slot, d], xdt.astype(BF16), preferred_element_type=F32)
        y_diag = y_all[(nh - 1) * q:nh * q]
        for j in range(nh - 2, -1, -1):
            y_diag = jnp.where(lane_head == j, y_all[j * q:(j + 1) * q], y_diag)
        st = st_ref[d]
        y_off = jnp.dot(cm, st.astype(BF16), preferred_element_type=F32) * jnp.exp(cs_x)
        cs_edge = cs_x[edge:edge + 1, :]
        xw = (xdt * jnp.exp(cs_edge - cs_x)).astype(BF16)
        st_ref[d] = jnp.exp(cs_edge) * st + jnp.dot(bmt_ref[slot, d], xw,
                                                    preferred_element_type=F32)
        yd_ref[d, pl.ds(t0, q), :] = y_diag + y_off

    st_ref[...] = h0_ref[0, :, 0]

    def times(c):
        return pl.multiple_of(c * q, q), pl.multiple_of((nchunk - 1 - c) * q, q)

    def prepare_pair(c, slot):
        tf, tb = times(c)
        prepare(tf, 0, slot)
        prepare(tb, 1, slot)

    def sweep_body(c, carry):
        slot = c % 2
        prepare_pair(jnp.minimum(c + 1, nchunk - 1), 1 - slot)
        tf, tb = times(c)
        consume(tf, 0, slot)
        consume(tb, 1, slot)
        return carry

    prepare_pair(0, 0)
    lax.fori_loop(0, nchunk, sweep_body, 0, unroll=2 if nchunk % 2 == 0 else 1)
    fin_ref[0, :, 0] = st_ref[...]
    dch = dch_ref[...]

    def combine_body(c, carry):
        t0 = pl.multiple_of(c * q, q)
        xs = xc_ref[pl.ds(t0, q), 0:gw].astype(F32)
        z = z_ref[0, pl.ds(t0, q), :].astype(F32)
        ys = yd_ref[0, pl.ds(t0, q), :] + yd_ref[1, pl.ds(t0, q), :] + dch * xs
        y_ref[0, pl.ds(t0, q), :] = (ys * jax.nn.silu(z)).astype(y_ref.dtype)
        return carry

    lax.fori_loop(0, nchunk, combine_body, 0)


def _split3(x):
    hi = x.astype(BF16)
    rest = x - hi.astype(F32)
    mid = rest.astype(BF16)
    return hi, mid, (rest - mid.astype(F32)).astype(BF16)


def _ssd_tables():
    q, nh, hp, gw = SSM_CHUNK, HEADS_PER_GROUP, SSM_HEADDIM, GROUP_W
    r = jnp.arange(q)
    lower = (r[:, None] >= r[None, :]).astype(BF16)
    tri = jnp.stack([jnp.tile(lower, (1, 3)), jnp.tile(lower.T, (1, 3))])
    src = jnp.arange(LANES)[:, None]
    col = jnp.arange(gw + nh * q)[None, :]
    head_of_col = jnp.where(col < gw, col // hp, (col - gw) // q)
    expand = jnp.stack([jnp.tile((src == d * nh + head_of_col).astype(BF16), (2, 1))
                        for d in range(2)])
    return tri, expand


def _ssd(proj, dt, p, h0):
    bsz, length, _ = proj.shape
    gw, ns, nh, hp = GROUP_W, SSM_STATE, HEADS_PER_GROUP, SSM_HEADDIM
    tri, expand = _ssd_tables()
    blk = lambda off: pl.BlockSpec((1, length, gw), lambda b, g: (b, 0, off // gw + g))
    vec = lambda rows, w: pl.BlockSpec((rows, w), lambda b, g: (0, g))
    full = lambda a: pl.BlockSpec(a.shape, lambda b, g: (0,) * a.ndim)
    st = pl.BlockSpec((1, 2, 1, ns, gw), lambda b, g: (b, 0, g, 0, 0))
    return pl.pallas_call(
        functools.partial(_ssd_kernel, length=length),
        grid=(bsz, SSM_GROUPS),
        in_specs=[blk(COL_XS), blk(COL_BC), blk(COL_Z),
                  pl.BlockSpec((1, length, LANES), lambda b, g: (b, 0, g)),
                  vec(4, gw), vec(1, gw), vec(4, 2 * ns), vec(1, 2 * ns),
                  vec(1, LANES), vec(1, LANES), vec(1, gw), full(tri), full(expand), st],
        out_specs=[pl.BlockSpec((1, length, gw), lambda b, g: (b, 0, g)), st],
        out_shape=[jax.ShapeDtypeStruct((bsz, length, BRANCH_W), BF16),
                   jax.ShapeDtypeStruct((bsz, 2, SSM_GROUPS, ns, gw), F32)],
        scratch_shapes=[pltpu.VMEM((length + 2 * PAD, gw), F32),
                        pltpu.VMEM((length + 2 * PAD, 2 * ns), F32),
                        pltpu.VMEM((length, gw + 2 * ns), BF16),
                        pltpu.VMEM((2, length, gw), F32),
                        pltpu.VMEM((2, ns, gw), F32),
                        pltpu.VMEM((2, 2, nh * SSM_CHUNK, SSM_CHUNK), BF16),
                        pltpu.VMEM((2, 2, SSM_CHUNK, gw), F32),
                        pltpu.VMEM((2, 2, SSM_CHUNK, gw), F32),
                        pltpu.VMEM((2, 2, ns, SSM_CHUNK), BF16)],
        compiler_params=_cp(("parallel", "parallel"), VMEM_BIG),
        name="ssd",
    )(proj, proj, proj, dt, p["ssm_cw_x"], p["ssm_cb_x"], p["ssm_cw_bc"], p["ssm_cb_bc"],
      p["ssm_dtb"], p["ssm_alog"], p["ssm_dch"], tri, expand, h0)


HY_CONV_CHUNK = 256


def _hy_conv_kernel(u_ref, w_ref, b_ref, o_ref, pad_ref, *, length):
    width = u_ref.shape[-1]
    tc = min(HY_CONV_CHUNK, length)
    _fill_padded(pad_ref, u_ref, length, width, tc)
    w, b = w_ref[...], b_ref[...]

    def body(c, carry):
        t0 = pl.multiple_of(c * tc, tc)
        o_ref[0, 0, pl.ds(t0, tc), :] = _conv_rows(pad_ref, t0, tc, w, b, 1).astype(o_ref.dtype)
        return carry

    lax.fori_loop(0, length // tc, body, 0)


def _hy_conv(proj, p):
    bsz, length, _ = proj.shape
    w = BRANCH_W
    return pl.pallas_call(
        functools.partial(_hy_conv_kernel, length=length),
        grid=(bsz, 3),
        in_specs=[pl.BlockSpec((1, length, w), lambda b, j: (b, 0, COL_HY // w + j)),
                  pl.BlockSpec((3, w), lambda b, j: (0, j)),
                  pl.BlockSpec((1, w), lambda b, j: (0, j))],
        out_specs=pl.BlockSpec((1, 1, length, w), lambda b, j: (j, b, 0, 0)),
        out_shape=jax.ShapeDtypeStruct((3, bsz, length, w), BF16),
        scratch_shapes=[pltpu.VMEM((length + 2 * PAD, w), F32)],
        compiler_params=_cp(("parallel", "parallel"), VMEM_BIG),
        name="hyena_short_conv",
    )(proj, p["hy_conv_w"], p["hy_conv_b"])


def _hy_filter_kernel(f_ref, w1_ref, b1_ref, w2_ref, b2_ref, w3_ref, fr_ref, dl_ref, o_ref):
    feats = f_ref[...]
    h = jnp.sin(fr_ref[0:1, :] * (jnp.dot(feats, w1_ref[...], precision=HIGHEST,
                                          preferred_element_type=F32) + b1_ref[...]))
    h = jnp.sin(fr_ref[1:2, :] * (jnp.dot(h, w2_ref[...], precision=HIGHEST,
                                          preferred_element_type=F32) + b2_ref[...]))
    h = jnp.dot(h, w3_ref[...], precision=HIGHEST, preferred_element_type=F32)
    window = jnp.exp(-feats[:, 0:1] * dl_ref[...])
    fwd = feats[:, HY_EMB:HY_EMB + 1] * window
    bwd = feats[:, HY_EMB + 1:HY_EMB + 2] * window
    w = BRANCH_W
    for o in range(HY_ORDER):
        o_ref[o] = h[:, 2 * o * w:(2 * o + 1) * w] * fwd + h[:, (2 * o + 1) * w:(2 * o + 2) * w] * bwd


def _hy_taps(length, n, p):
    pos = jnp.arange(length, dtype=F32)[:, None]
    t01 = jnp.linspace(0.0, 1.0, length, dtype=F32)[:, None]
    bands = jnp.linspace(1e-4, HY_BANDS - 1, HY_BANDS, dtype=F32)
    ang = (2.0 * math.pi / length) * pos * bands
    feats = jnp.concatenate([t01, jnp.cos(ang), jnp.sin(ang)], axis=-1)

    def flagged(f, col):
        flags = jnp.zeros((f.shape[0], LANES - HY_EMB), F32).at[:, col].set(1.0)
        return jnp.concatenate([f, flags], axis=1)

    feats = jnp.concatenate([flagged(feats, 0), jnp.zeros((n - 2 * length + 1, LANES), F32),
                             flagged(feats[:0:-1], 1)], axis=0)
    max_decay = math.log(HY_TARGET) / HY_FAST_DECAY
    min_decay = math.log(HY_TARGET) / HY_SLOW_DECAY
    deltas = jnp.abs(jnp.linspace(min_decay, max_decay, BRANCH_W, dtype=F32))[None, :]
    tl = 512
    ncol = HY_ORDER * 2 * BRANCH_W
    full = lambda r, c: pl.BlockSpec((r, c), lambda i: (0, 0))
    return pl.pallas_call(
        _hy_filter_kernel,
        grid=(n // tl,),
        in_specs=[pl.BlockSpec((tl, LANES), lambda i: (i, 0)),
                  full(LANES, HY_FFN), full(1, HY_FFN), full(HY_FFN, HY_FFN), full(1, HY_FFN),
                  full(HY_FFN, ncol), full(2, HY_FFN), full(1, BRANCH_W)],
        out_specs=pl.BlockSpec((HY_ORDER, tl, BRANCH_W), lambda i: (0, i, 0)),
        out_shape=jax.ShapeDtypeStruct((HY_ORDER, n, BRANCH_W), F32),
        compiler_params=_cp(("parallel",)),
        name="hyena_filters",
    )(feats, p["hy_w1"], p["hy_b1"], p["hy_w2"], p["hy_b2"], p["hy_w3"], p["hy_freq"], deltas)


def _dft_tables(n1):
    n2 = FFT_N2
    n = n1 * n2
    two_pi = 2.0 * math.pi

    def cs(idx, period):
        ang = (two_pi / period) * (idx % period).astype(F32)
        return jnp.cos(ang), jnp.sin(ang)

    k1 = jnp.arange(n1, dtype=jnp.int32)
    c, s = cs(k1[:, None] * k1[None, :], n1)
    half = n1 // 2
    ch, sh = c[:, :half], s[:, :half]
    m1_re = jnp.concatenate([ch, sh], axis=1)
    m1_im = jnp.concatenate([-sh, ch], axis=1)
    m1 = jnp.stack([m1_re, m1_im], axis=1).reshape(2 * n1, n1)
    m1_taps = jnp.stack([c, -s], axis=1).reshape(2 * n1, n1)
    ci, si = ch.T / n, sh.T / n
    m2_re = jnp.stack([ci, -si], axis=2).reshape(half, 2 * n1)
    m2_im = jnp.stack([si, ci], axis=2).reshape(half, 2 * n1)
    m2 = jnp.concatenate([m2_re, m2_im], axis=0)
    k2 = jnp.arange(n2, dtype=jnp.int32)
    idx = k2[None, None, :] * (k2[None, :, None] * n1 + k1[:, None, None])
    gc, gs = cs(idx, n)
    g_fwd = jnp.concatenate([jnp.concatenate([gc, gs], axis=2),
                             jnp.concatenate([-gs, gc], axis=2)], axis=1)
    gct, gst = jnp.swapaxes(gc, 1, 2), jnp.swapaxes(gs, 1, 2)
    g_inv = jnp.concatenate([jnp.concatenate([gct, -gst], axis=2),
                             jnp.concatenate([gst, gct], axis=2)], axis=1)
    return m1, m1_taps, m2, g_fwd, g_inv


def _stage1_kernel(m_ref, *refs, nin, prec):
    o_ref = refs[-1]
    parts = [r[0] for r in refs[:nin]]
    x = parts[0] if nin == 1 else jnp.concatenate(parts, axis=0)
    o_ref[0] = jnp.dot(m_ref[...], x, precision=prec,
                       preferred_element_type=F32).astype(o_ref.dtype)


def _fft_chunk(ncols):
    return min(8192, ncols)


FFT_SUB = 16
FFT_NSUB = FFT_N2 // FFT_SUB


def _stage1_data_kernel(m_ref, x_ref, o_ref):
    rows = x_ref.shape[0] * x_ref.shape[1] * FFT_SUB
    x = x_ref[...].reshape(rows, x_ref.shape[-1])
    t = jnp.dot(m_ref[...], x, preferred_element_type=F32)
    o_ref[0, 0] = t.astype(o_ref.dtype).reshape(o_ref.shape[2:])


def _stage1_data(zsrc, zi, mbig1):
    nsrc, bsz, lp, w = zsrc.shape
    half = lp // FFT_N2
    npair = bsz // 2
    zv = zsrc.reshape(nsrc, bsz, half, FFT_NSUB, FFT_SUB, w)
    return pl.pallas_call(
        _stage1_data_kernel,
        grid=(npair, FFT_NSUB),
        in_specs=[pl.BlockSpec(mbig1.shape, lambda p, j: (0, 0)),
                  pl.BlockSpec((None, 2, half, None, FFT_SUB, w),
                               lambda p, j: (zi, p, 0, j, 0, 0))],
        out_specs=pl.BlockSpec((1, 1, 4 * half, FFT_SUB, w), lambda p, j: (p, j, 0, 0, 0)),
        out_shape=jax.ShapeDtypeStruct((npair, FFT_NSUB, 4 * half, FFT_SUB, w), BF16),
        compiler_params=_cp(("parallel", "parallel"), 48 * 1024 * 1024),
        name="dft_stage1",
    )(mbig1, zv)


def _stage1_taps(taps, m1t):
    nord, n1, ncols = taps.shape
    ch = _fft_chunk(ncols)
    return pl.pallas_call(
        functools.partial(_stage1_kernel, nin=1, prec=HIGHEST),
        grid=(nord, ncols // ch),
        in_specs=[pl.BlockSpec(m1t.shape, lambda p, j: (0, 0)),
                  pl.BlockSpec((1, n1, ch), lambda p, j: (p, 0, j))],
        out_specs=pl.BlockSpec((1, 2 * n1, ch), lambda p, j: (p, 0, j)),
        out_shape=jax.ShapeDtypeStruct((nord, 2 * n1, ncols), F32),
        compiler_params=_cp(("parallel", "parallel")),
        name="dft_stage1_taps",
    )(m1t, taps)


def _mid_taps_kernel(g_ref, t_ref, o_ref):
    o_ref[0, 0] = jnp.dot(g_ref[0], t_ref[0, 0], precision=HIGHEST, preferred_element_type=F32)


def _mid_taps(t, g_fwd):
    nord, n1, rows, w = t.shape
    return pl.pallas_call(
        _mid_taps_kernel,
        grid=(n1, nord),
        in_specs=[pl.BlockSpec((1, rows, rows), lambda k, o: (k, 0, 0)),
                  pl.BlockSpec((1, 1, rows, w), lambda k, o: (o, k, 0, 0))],
        out_specs=pl.BlockSpec((1, 1, rows, w), lambda k, o: (o, k, 0, 0)),
        out_shape=jax.ShapeDtypeStruct(t.shape, F32),
        compiler_params=_cp(("parallel", "parallel")),
        name="dft_mid_taps",
    )(g_fwd, t)


def _mid_kernel(gf_ref, gi_ref, k_ref, t_ref, o_ref):
    n2 = FFT_N2
    w = t_ref.shape[-1]
    kr, ki = k_ref[0, :n2], k_ref[0, n2:]
    for p in range(t_ref.shape[0]):
        t = jnp.concatenate([t_ref[p, :, 0, 0].reshape(n2, w),
                             t_ref[p, :, 0, 1].reshape(n2, w)], axis=0)
        z = jnp.dot(gf_ref[0], t, preferred_element_type=F32)
        zr, zi = z[:n2], z[n2:]
        y = jnp.concatenate([zr * kr - zi * ki, zr * ki + zi * kr], axis=0).astype(BF16)
        u = jnp.dot(gi_ref[0], y, preferred_element_type=F32).astype(o_ref.dtype)
        o_ref[p, :, 0, 0] = u[:n2].reshape(FFT_NSUB, FFT_SUB, w)
        o_ref[p, :, 0, 1] = u[n2:].reshape(FFT_NSUB, FFT_SUB, w)


def _mid(t, kspec, g_fwd, g_inv):
    npair, nsub, rows, sub, w = t.shape
    n1 = rows // 2
    tv = t.reshape(npair, nsub, n1, 2, sub, w)
    gspec = pl.BlockSpec((1, 2 * FFT_N2, 2 * FFT_N2), lambda k: (k, 0, 0))
    tspec = pl.BlockSpec((npair, nsub, 1, 2, sub, w), lambda k: (0, 0, k, 0, 0, 0))
    out = pl.pallas_call(
        _mid_kernel,
        grid=(n1,),
        in_specs=[gspec, gspec,
                  pl.BlockSpec((1, 2 * FFT_N2, w), lambda k: (k, 0, 0)),
                  tspec],
        out_specs=tspec,
        out_shape=jax.ShapeDtypeStruct(tv.shape, BF16),
        compiler_params=_cp(("parallel",)),
        name="dft_mid",
    )(g_fwd, g_inv, kspec, tv)
    return out.reshape(t.shape)


def _last_kernel(m_ref, u_ref, z_ref, x_ref, bias_ref, o_ref):
    w = u_ref.shape[-1]
    u = u_ref[0, 0].reshape(u_ref.shape[2] * FFT_SUB, w)
    y = jnp.dot(m_ref[...], u, preferred_element_type=F32)
    rows = y.shape[0]
    z = z_ref[...].reshape(rows, w)
    x = x_ref[...].reshape(rows, w)
    conv = (y + z.astype(F32) * bias_ref[...]).astype(BF16)
    o_ref[...] = (x * conv).astype(o_ref.dtype).reshape(o_ref.shape)


def _last(u, zsrc, zi, xsrc, xi, bias_row, mbig2):
    npair, nsub, rows, sub, w = u.shape
    half = rows // 4
    bsz, lp = 2 * npair, half * FFT_N2
    view = lambda a: a.reshape(a.shape[0], bsz, half, nsub, sub, w)
    tok = lambda k: pl.BlockSpec((None, 2, half, None, sub, w), lambda p, j: (k, p, 0, j, 0, 0))
    out = pl.pallas_call(
        _last_kernel,
        grid=(npair, nsub),
        in_specs=[pl.BlockSpec(mbig2.shape, lambda p, j: (0, 0)),
                  pl.BlockSpec((1, 1, rows, sub, w), lambda p, j: (p, j, 0, 0, 0)),
                  tok(zi), tok(xi),
                  pl.BlockSpec((1, w), lambda p, j: (0, 0))],
        out_specs=tok(0),
        out_shape=jax.ShapeDtypeStruct((1, bsz, half, nsub, sub, w), BF16),
        compiler_params=_cp(("parallel", "parallel"), 48 * 1024 * 1024),
        name="dft_last",
    )(mbig2, u, view(zsrc), view(xsrc), bias_row)
    return out.reshape(1, bsz, lp, w)


def _hyena_spectra(length, p):
    lp = max(length, FFT_MIN_LEN)
    n1 = 2 * lp // FFT_N2
    n = 2 * lp
    w = BRANCH_W
    taps = _hy_taps(length, n, p)
    m1, m1t, m2, g_fwd, g_inv = _dft_tables(n1)
    t = _stage1_taps(taps.reshape(HY_ORDER, n1, FFT_N2 * w), m1t)
    kspec = _mid_taps(t.reshape(HY_ORDER, n1, 2 * FFT_N2, w), g_fwd)
    eye = jnp.eye(FFT_SUB, dtype=F32)
    return kspec, (jnp.kron(m1, eye).astype(BF16), jnp.kron(m2, eye).astype(BF16),
                   g_fwd.astype(BF16), g_inv.astype(BF16))


def _hyena(proj, p, kspec, tables):
    bsz, length, _ = proj.shape
    mbig1, mbig2, g_fwd, g_inv = tables
    lp = mbig2.shape[0] // (2 * FFT_SUB) * FFT_N2
    uc = _hy_conv(proj, p)
    if lp != length:
        uc = jnp.pad(uc, ((0, 0), (0, 0), (0, lp - length), (0, 0)))
    zsrc, zi = uc, 0
    for order in range(HY_ORDER):
        t = _stage1_data(zsrc, zi, mbig1)
        u = _mid(t, kspec[order], g_fwd, g_inv)
        zsrc, zi = _last(u, zsrc, zi, uc, order + 1, p["hy_bias"][order:order + 1], mbig2), 0
    return zsrc[0, :, :length]


def _merge_kernel(x_ref, ya_ref, yb_ref, yc_ref, g0_ref, g1_ref, g2_ref, wb_ref, wo_ref,
                  sn_ref, n1_ref, gate_ref, o_ref):
    yc = _rms(yc_ref[0].astype(F32), sn_ref[...]).astype(BF16)
    merged = None
    for k, (y, g_ref) in enumerate(((ya_ref[0], g0_ref), (yb_ref[0], g1_ref), (yc, g2_ref))):
        term = jax.nn.sigmoid(g_ref[0].astype(F32)) * jnp.dot(y, wb_ref[k],
                                                              preferred_element_type=F32)
        merged = term if merged is None else merged + term
    out = jnp.dot(merged.astype(BF16), wo_ref[...], preferred_element_type=F32)
    o_ref[0] = x_ref[0] + gate_ref[0] * _rms(out, n1_ref[...])


def _merge(x, proj, ya, yb, yc, p, norm1, mod, row_of, k_gate):
    bsz, length, _ = x.shape
    tm = min(512, length)
    w = BRANCH_W
    tok = lambda c: pl.BlockSpec((1, tm, c), lambda b, i: (b, i, 0))
    gspec = lambda k: pl.BlockSpec((1, tm, D_MODEL),
                                   lambda b, i: (b, i, COL_GATES // D_MODEL + k))
    return pl.pallas_call(
        _merge_kernel,
        grid=(bsz, length // tm),
        in_specs=[tok(D_MODEL), tok(w), tok(w), tok(w), gspec(0), gspec(1), gspec(2),
                  pl.BlockSpec((N_BRANCH, w, D_MODEL), lambda b, i: (0, 0, 0)),
                  pl.BlockSpec((D_MODEL, D_MODEL), lambda b, i: (0, 0)),
                  pl.BlockSpec((1, w), lambda b, i: (0, 0)),
                  pl.BlockSpec((1, D_MODEL), lambda b, i: (0, 0)),
                  pl.BlockSpec((1, 1, D_MODEL), lambda b, i: (row_of(b), 0, k_gate))],
        out_specs=tok(D_MODEL),
        out_shape=jax.ShapeDtypeStruct(x.shape, F32),
        compiler_params=_cp(("parallel", "parallel"), VMEM_BIG),
        name="merge_out",
    )(x, ya, yb, yc, proj, proj, proj, p["w_branch"], p["w_out"], p["ssm_norm"],
      norm1.reshape(1, D_MODEL), mod)


FFN_CH = 256


def _ffn_kernel(x_ref, xp_ref, xn_ref, n2_ref, sh_ref, sc_ref, wu_ref, cw_ref, cb_ref, wd_ref,
                n3_ref, gate_ref, o_ref, hx_ref, h_ref, *, grid_w, tm):
    i = pl.program_id(1)
    nt = pl.num_programs(1)
    rows = tm + 2 * grid_w
    g2, sc, sh = n2_ref[...], sc_ref[0], sh_ref[0]

    def modulated(x):
        return _rms(x, g2) * (1.0 + sc) + sh

    hx_ref[0:grid_w] = (modulated(xp_ref[0]) * (i > 0).astype(F32)).astype(BF16)
    hx_ref[grid_w:grid_w + tm] = modulated(x_ref[0]).astype(BF16)
    hx_ref[grid_w + tm:rows] = (modulated(xn_ref[0]) * (i < nt - 1).astype(F32)).astype(BF16)

    col = lax.broadcasted_iota(jnp.int32, (rows, FFN_CH), 0) % grid_w
    not_first = col != 0
    not_last = col != grid_w - 1

    def conv(c0):
        ext = jnp.dot(hx_ref[...], wu_ref[:, c0:c0 + FFN_CH], preferred_element_type=F32)
        left = jnp.where(not_first, pltpu.roll(ext, 1, 0), 0.0).astype(BF16)
        right = jnp.where(not_last, pltpu.roll(ext, rows - 1, 0), 0.0).astype(BF16)
        mid = ext.astype(BF16)
        acc = cb_ref[:, c0:c0 + FFN_CH].astype(BF16)
        for dr in range(3):
            base = dr * grid_w
            for dc, src in enumerate((left, mid, right)):
                k = dr * 3 + dc
                acc = acc + cw_ref[k:k + 1, c0:c0 + FFN_CH].astype(BF16) * src[base:base + tm]
        return acc

    for j in range(FFN_HIDDEN // FFN_CH):
        val = conv(j * FFN_CH)
        gate = conv(FFN_HIDDEN + j * FFN_CH)
        h_ref[:, j * FFN_CH:(j + 1) * FFN_CH] = jax.nn.gelu(gate) * val
    down = jnp.dot(h_ref[...], wd_ref[...], preferred_element_type=F32)
    o_ref[0] = x_ref[0] + gate_ref[0] * _rms(down, n3_ref[...])


def _ffn(x, p, norm2, norm3, mod, row_of, k_shift, grid_w):
    bsz, length, _ = x.shape
    tm = min(512, length)
    ncol = 2 * FFN_HIDDEN
    per = tm // grid_w
    nblk = length // grid_w
    const = lambda shape: pl.BlockSpec(shape, lambda b, i: (0,) * len(shape),
                                       pipeline_mode=pl.Buffered(1))
    modv = lambda k: pl.BlockSpec((1, 1, D_MODEL), lambda b, i: (row_of(b), 0, k))
    return pl.pallas_call(
        functools.partial(_ffn_kernel, grid_w=grid_w, tm=tm),
        grid=(bsz, length // tm),
        in_specs=[pl.BlockSpec((1, tm, D_MODEL), lambda b, i: (b, i, 0)),
                  pl.BlockSpec((1, grid_w, D_MODEL),
                               lambda b, i: (b, jnp.maximum(i * per - 1, 0), 0)),
                  pl.BlockSpec((1, grid_w, D_MODEL),
                               lambda b, i: (b, jnp.minimum((i + 1) * per, nblk - 1), 0)),
                  const((1, D_MODEL)), modv(k_shift), modv(k_shift + 1),
                  const((D_MODEL, ncol)), const((9, ncol)), const((1, ncol)),
                  const((FFN_HIDDEN, D_MODEL)), const((1, D_MODEL)), modv(k_shift + 2)],
        out_specs=pl.BlockSpec((1, tm, D_MODEL), lambda b, i: (b, i, 0)),
        out_shape=jax.ShapeDtypeStruct(x.shape, F32),
        scratch_shapes=[pltpu.VMEM((tm + 2 * grid_w, D_MODEL), BF16),
                        pltpu.VMEM((tm, FFN_HIDDEN), BF16)],
        compiler_params=_cp(("parallel", "parallel"), VMEM_BIG),
        name="ffn",
    )(x, x, x, norm2.reshape(1, D_MODEL), mod, mod, p["ffn_w_up"], p["ffn_conv_w"],
      p["ffn_conv_b"], p["ffn_w_down"], norm3.reshape(1, D_MODEL), mod)


def _block_diag(w):
    nb, k, j = w.shape
    eye = jnp.eye(nb, dtype=w.dtype)
    return (eye[:, None, :, None] * w[:, :, None, :]).reshape(nb * k, nb * j)


def _layer_params(l, a):
    ns = SSM_STATE
    w_in = a["w_in"][l]
    xbc0 = BRANCH_W
    b0 = xbc0 + BRANCH_W
    c0 = b0 + SSM_BC
    bc_cols = jnp.concatenate([w_in[:, b0:b0 + ns], w_in[:, c0:c0 + ns],
                               w_in[:, b0 + ns:b0 + 2 * ns], w_in[:, c0 + ns:c0 + 2 * ns]], axis=1)
    out0 = STATE_COLS
    w_main = jnp.concatenate([
        w_in[:, :BRANCH_W],
        w_in[:, xbc0:xbc0 + BRANCH_W],
        bc_cols,
        w_in[:, out0:out0 + BRANCH_W],
        w_in[:, out0 + BRANCH_W:out0 + BRANCH_W + HY_PROJ],
        w_in[:, out0 + BRANCH_W + HY_PROJ:out0 + 2 * BRANCH_W + HY_PROJ],
        w_in[:, out0 + 2 * BRANCH_W + HY_PROJ:],
    ], axis=1).astype(BF16)
    dt0 = BRANCH_W + SSM_XBC

    def per_group(v2):
        rows = []
        for g in range(SSM_GROUPS):
            sel = v2[:, g * HEADS_PER_GROUP:(g + 1) * HEADS_PER_GROUP].reshape(-1)
            rows.append(jnp.pad(sel, (0, LANES - sel.shape[0])))
        return jnp.concatenate(rows)[None, :]

    w_dt_cols = w_in[:, dt0:dt0 + 2 * SSM_HEADS].reshape(D_MODEL, 2, SSM_HEADS)
    w_dt = jnp.concatenate([
        jnp.pad(w_dt_cols[:, :, g * HEADS_PER_GROUP:(g + 1) * HEADS_PER_GROUP].reshape(D_MODEL, -1),
                ((0, 0), (0, LANES - 2 * HEADS_PER_GROUP))) for g in range(SSM_GROUPS)],
        axis=1).astype(BF16)
    scw = a["ssm_conv_w"][l]
    scb = a["ssm_conv_b"][l][None, :]
    perm_bc = lambda v: jnp.concatenate(
        [v[:, BRANCH_W:BRANCH_W + ns], v[:, BRANCH_W + SSM_BC:BRANCH_W + SSM_BC + ns],
         v[:, BRANCH_W + ns:BRANCH_W + 2 * ns], v[:, BRANCH_W + SSM_BC + ns:]], axis=1)
    return {
        "w_main": w_main, "w_dt": w_dt,
        "lru_conv_w": a["lru_conv_w"][l], "lru_conv_b": a["lru_conv_b"][l][None, :],
        "lru_waf": _block_diag(a["lru_w_a"][l, 0]).astype(BF16),
        "lru_wif": _block_diag(a["lru_w_i"][l, 0]).astype(BF16),
        "lru_wab": _block_diag(a["lru_w_a"][l, 1]).astype(BF16),
        "lru_wib": _block_diag(a["lru_w_i"][l, 1]).astype(BF16),
        "lru_bias": jnp.stack([a["lru_b_a"][l, 0], a["lru_b_i"][l, 0],
                               a["lru_b_a"][l, 1], a["lru_b_i"][l, 1]]),
        "lru_lambda": a["lru_lambda"][l],
        "hy_conv_w": a["hy_conv_w"][l], "hy_conv_b": a["hy_conv_b"][l][None, :],
        "hy_w1": jnp.pad(a["hy_w1"][l], ((0, LANES - HY_EMB), (0, 0))),
        "hy_b1": a["hy_b1"][l][None, :], "hy_w2": a["hy_w2"][l], "hy_b2": a["hy_b2"][l][None, :],
        "hy_w3": a["hy_w3"][l], "hy_freq": a["hy_freq"][l], "hy_bias": a["hy_bias"][l],
        "ssm_cw_x": scw[:, :BRANCH_W], "ssm_cb_x": scb[:, :BRANCH_W],
        "ssm_cw_bc": perm_bc(scw), "ssm_cb_bc": perm_bc(scb),
        "ssm_dtb": per_group(a["ssm_dt_bias"][l]), "ssm_alog": per_group(a["ssm_a_log"][l]),
        "ssm_dch": jnp.repeat(a["ssm_d"][l], SSM_HEADDIM)[None, :],
        "ssm_norm": a["ssm_norm"][l][None, :],
        "w_branch": a["w_branch"][l].astype(BF16), "w_out": a["w_out"][l].astype(BF16),
        "ffn_w_up": a["ffn_w_up"][l].astype(BF16),
        "ffn_conv_w": a["ffn_conv_w"][l].reshape(9, 2 * FFN_HIDDEN),
        "ffn_conv_b": a["ffn_conv_b"][l][None, :],
        "ffn_w_down": a["ffn_w_down"][l].astype(BF16),
    }


def _mixer_states(tokens, p, norm0, mod, row_of, base, init):
    proj, dt = _norm_mm(tokens, norm0, mod, row_of, base, base + 1, p["w_main"], p["w_dt"],
                        tn=MAIN_COLS // 4)
    ya, lru_fin = _lru(proj, p, init[0])
    yc, ssm_fin = _ssd(proj, dt, p, init[1])
    return proj, ya, yc, (lru_fin, ssm_fin)


def _layer_tokens(tokens, p, norms_l, mod, row_of, init, hy, grid_w):
    proj, ya, yc, finals = _mixer_states(tokens, p, norms_l[0], mod, row_of, 0, init)
    yb = _hyena(proj, p, *hy)
    tokens = _merge(tokens, proj, ya, yb, yc, p, norms_l[1], mod, row_of, 2)
    tokens = _ffn(tokens, p, norms_l[2], norms_l[3], mod, row_of, 3, grid_w)
    return tokens, finals


def kernel(x, c, ctx, c_ctx, mod_w, mod_b, norms, w_in, lru_conv_w, lru_conv_b, lru_w_a, lru_b_a,
           lru_w_i, lru_b_i, lru_lambda, hy_conv_w, hy_conv_b, hy_w1, hy_b1, hy_w2, hy_b2, hy_w3,
           hy_freq, hy_bias, ssm_conv_w, ssm_conv_b, ssm_dt_bias, ssm_a_log, ssm_d, ssm_norm,
           w_branch, w_out, ffn_w_up, ffn_conv_w, ffn_conv_b, ffn_w_down):
    a = dict(w_in=w_in, lru_conv_w=lru_conv_w, lru_conv_b=lru_conv_b, lru_w_a=lru_w_a,
             lru_b_a=lru_b_a, lru_w_i=lru_w_i, lru_b_i=lru_b_i, lru_lambda=lru_lambda,
             hy_conv_w=hy_conv_w, hy_conv_b=hy_conv_b, hy_w1=hy_w1, hy_b1=hy_b1, hy_w2=hy_w2,
             hy_b2=hy_b2, hy_w3=hy_w3, hy_freq=hy_freq, hy_bias=hy_bias, ssm_conv_w=ssm_conv_w,
             ssm_conv_b=ssm_conv_b, ssm_dt_bias=ssm_dt_bias, ssm_a_log=ssm_a_log, ssm_d=ssm_d,
             ssm_norm=ssm_norm, w_branch=w_branch, w_out=w_out, ffn_w_up=ffn_w_up,
             ffn_conv_w=ffn_conv_w, ffn_conv_b=ffn_conv_b, ffn_w_down=ffn_w_down)
    bsz, length, _ = x.shape
    ctx_len = ctx.shape[1]
    depth = mod_w.shape[0]
    rows = -(-(bsz + 1) // 8) * 8
    cc = jnp.concatenate([c, c_ctx[None, :], jnp.zeros((rows - bsz - 1, D_MODEL), F32)], axis=0)
    mod_all = _modulation(cc, mod_w, mod_b)
    lat_row = lambda b: b
    ctx_row = lambda b: bsz
    zero_init = (jnp.zeros((bsz, 2, BRANCH_W), F32),
                 jnp.zeros((bsz, 2, SSM_GROUPS, SSM_STATE, GROUP_W), F32))
    for l in range(depth):
        last = l == depth - 1
        p = _layer_params(l, a)
        mod = mod_all[l].reshape(rows, 1, 6 * D_MODEL)
        if last:
            _, _, _, ctx_states = _mixer_states(ctx, p, norms[l, 0], mod, ctx_row, 0, zero_init)
        else:
            hy_ctx = _hyena_spectra(ctx_len, p)
            ctx, ctx_states = _layer_tokens(ctx, p, norms[l], mod, ctx_row, zero_init, hy_ctx,
                                            ctx_len)
        hy_lat = _hyena_spectra(length, p)
        x, _ = _layer_tokens(x, p, norms[l], mod, lat_row, ctx_states, hy_lat, GRID_W)
    return x
```

```python
import functools
import math

import jax
import jax.numpy as jnp
from jax import lax
from jax.experimental import pallas as pl
from jax.experimental.pallas import tpu as pltpu

F32 = jnp.float32
BF16 = jnp.bfloat16
HIGHEST = lax.Precision.HIGHEST

D_MODEL = 1024
DEPTH = 2
GRID_W = 64
BRANCH_W = 512
N_BRANCH = 3
LRU_BLOCKS = 8
LRU_BLOCK_W = BRANCH_W // LRU_BLOCKS
LRU_C = 8.0
HY_ORDER = 2
HY_PROJ = (HY_ORDER + 1) * BRANCH_W
HY_BANDS = 16
HY_EMB = 1 + 2 * HY_BANDS
HY_FFN = 64
HY_TARGET = 1e-2
HY_FAST_DECAY = 0.3
HY_SLOW_DECAY = 1.5
SSM_HEADDIM = 64
SSM_HEADS = BRANCH_W // SSM_HEADDIM
SSM_GROUPS = 2
SSM_STATE = 128
SSM_CHUNK = 128
SSM_BC = SSM_GROUPS * SSM_STATE
SSM_XBC = BRANCH_W + 2 * SSM_BC
FFN_HIDDEN = 2816
RMS_EPS = 1e-6
STATE_COLS = BRANCH_W + SSM_XBC + 2 * SSM_HEADS

COL_XA = 0
COL_XS = 512
COL_BC = 1024
COL_GA = 1536
COL_HY = 2048
COL_Z = 3584
COL_GATES = 4096
MAIN_COLS = 7168
HEADS_PER_GROUP = SSM_HEADS // SSM_GROUPS
GROUP_W = BRANCH_W // SSM_GROUPS
LANES = 128
FFT_N2 = 128
FFT_MIN_LEN = 1024
VMEM_BIG = 56 * 1024 * 1024
VMEM_MID = 48 * 1024 * 1024


def _cp(sem, vmem=None):
    return pltpu.CompilerParams(dimension_semantics=sem, vmem_limit_bytes=vmem)


def _rms(u, g):
    return u * lax.rsqrt(jnp.mean(u * u, axis=-1, keepdims=True) + RMS_EPS) * g


def _softplus(x):
    return jnp.maximum(x, 0.0) + jnp.log1p(jnp.exp(-jnp.abs(x)))


def _mod_kernel(c_ref, w_ref, b_ref, o_ref):
    c = c_ref[...]
    o_ref[0] = jnp.dot(jax.nn.silu(c), w_ref[0], precision=HIGHEST,
                       preferred_element_type=F32) + b_ref[0]


def _modulation(cc, mod_w, mod_b):
    rows = cc.shape[0]
    depth, _, ncol = mod_w.shape
    tn = 1024
    return pl.pallas_call(
        _mod_kernel,
        grid=(depth, ncol // tn),
        in_specs=[pl.BlockSpec((rows, D_MODEL), lambda l, j: (0, 0)),
                  pl.BlockSpec((1, D_MODEL, tn), lambda l, j: (l, 0, j)),
                  pl.BlockSpec((1, 1, tn), lambda l, j: (l, 0, j))],
        out_specs=pl.BlockSpec((1, rows, tn), lambda l, j: (l, 0, j)),
        out_shape=jax.ShapeDtypeStruct((depth, rows, ncol), F32),
        compiler_params=_cp(("parallel", "parallel")),
        name="modulation",
    )(cc, mod_w, mod_b.reshape(depth, 1, ncol))


def _norm_mm_kernel(x_ref, g_ref, sh_ref, sc_ref, w_ref, *rest, has_dt):
    if has_dt:
        wdt_ref, o_ref, odt_ref, hx_ref = rest
    else:
        o_ref, hx_ref = rest

    @pl.when(pl.program_id(2) == 0)
    def _():
        h = _rms(x_ref[0], g_ref[...]) * (1.0 + sc_ref[0]) + sh_ref[0]
        hx_ref[...] = h.astype(BF16)
        if has_dt:
            odt_ref[0] = jnp.dot(hx_ref[...], wdt_ref[...], preferred_element_type=F32)

    o_ref[0] = jnp.dot(hx_ref[...], w_ref[...], preferred_element_type=F32).astype(o_ref.dtype)


def _norm_mm(x, gnorm, mod, row_of, k_shift, k_scale, w, w_dt=None, tn=1024):
    bsz, length, _ = x.shape
    ncol = w.shape[1]
    tm = min(1024, length)
    has_dt = w_dt is not None
    in_specs = [pl.BlockSpec((1, tm, D_MODEL), lambda b, i, j: (b, i, 0)),
                pl.BlockSpec((1, D_MODEL), lambda b, i, j: (0, 0)),
                pl.BlockSpec((1, 1, D_MODEL), lambda b, i, j: (row_of(b), 0, k_shift)),
                pl.BlockSpec((1, 1, D_MODEL), lambda b, i, j: (row_of(b), 0, k_scale)),
                pl.BlockSpec((D_MODEL, tn), lambda b, i, j: (0, j))]
    args = [x, gnorm.reshape(1, D_MODEL), mod, mod, w]
    out_specs = [pl.BlockSpec((1, tm, tn), lambda b, i, j: (b, i, j))]
    out_shape = [jax.ShapeDtypeStruct((bsz, length, ncol), BF16)]
    if has_dt:
        ndt = w_dt.shape[1]
        in_specs.append(pl.BlockSpec((D_MODEL, ndt), lambda b, i, j: (0, 0)))
        args.append(w_dt)
        out_specs.append(pl.BlockSpec((1, tm, ndt), lambda b, i, j: (b, i, 0)))
        out_shape.append(jax.ShapeDtypeStruct((bsz, length, ndt), F32))
    res = pl.pallas_call(
        functools.partial(_norm_mm_kernel, has_dt=has_dt),
        grid=(bsz, length // tm, ncol // tn),
        in_specs=in_specs, out_specs=out_specs, out_shape=out_shape,
        scratch_shapes=[pltpu.VMEM((tm, D_MODEL), BF16)],
        compiler_params=_cp(("parallel", "parallel", "arbitrary"), VMEM_BIG),
        name="norm_proj",
    )(*args)
    return res if has_dt else res[0]


PAD = 8


def _fill_padded(pad_ref, src_ref, length, width, chunk):
    zeros = jnp.zeros((PAD, width), F32)
    pad_ref[pl.ds(0, PAD), :] = zeros
    pad_ref[pl.ds(PAD + length, PAD), :] = zeros

    def body(i, carry):
        t0 = pl.multiple_of(i * chunk, chunk)
        pad_ref[pl.ds(PAD + t0, chunk), :] = src_ref[0, pl.ds(t0, chunk), :].astype(F32)
        return carry

    lax.fori_loop(0, length // chunk, body, 0)


def _conv_rows(pad_ref, t0, rows, w, b, left):
    win = pad_ref[pl.ds(t0, rows + 2 * PAD), :]
    acc = b
    for k in range(w.shape[0]):
        off = PAD + k - left
        acc = acc + w[k:k + 1, :] * win[off:off + rows, :]
    return acc


LRU_CHUNK = 256


SUBLANES = 8


def _chunk_scan(a, b, h, reverse):
    rows, width = a.shape
    groups = rows // SUBLANES
    a3 = a.reshape(groups, SUBLANES, width)
    b3 = b.reshape(groups, SUBLANES, width)
    sub = lax.broadcasted_iota(jnp.int32, a3.shape, 1)
    s = 1
    while s < SUBLANES:
        keep = sub < SUBLANES - s if reverse else sub >= s
        shift = SUBLANES - s if reverse else s
        a_s = jnp.where(keep, pltpu.roll(a3, shift, 1), 1.0)
        b_s = jnp.where(keep, pltpu.roll(b3, shift, 1), 0.0)
        b3 = a3 * b_s + b3
        a3 = a3 * a_s
        s *= 2
    out = [None] * groups
    for i in (range(groups - 1, -1, -1) if reverse else range(groups)):
        hs = a3[i] * h + b3[i]
        out[i] = hs
        h = hs[0:1] if reverse else hs[SUBLANES - 1:SUBLANES]
    return jnp.concatenate(out, axis=0), h


def _lru_kernel(xa_ref, ga_ref, cw_ref, cb_ref, waf_ref, wif_ref, wab_ref, wib_ref,
                bias_ref, lam_ref, h0_ref, y_ref, fin_ref, pad_ref, hf_ref, *, length):
    width = xa_ref.shape[-1]
    tc = min(LRU_CHUNK, length)
    nchunk = length // tc
    _fill_padded(pad_ref, xa_ref, length, width, tc)
    cw = cw_ref[...]
    cb = cb_ref[...]
    neg_c_sp = -LRU_C * _softplus(-lam_ref[...])

    def gated(t0, wa_ref, wi_ref, d):
        xc = _conv_rows(pad_ref, t0, tc, cw, cb, 2)
        xcb = xc.astype(BF16)
        r = jax.nn.sigmoid(jnp.dot(xcb, wa_ref[...], preferred_element_type=F32)
                           + bias_ref[2 * d:2 * d + 1, :])
        i = jax.nn.sigmoid(jnp.dot(xcb, wi_ref[...], preferred_element_type=F32)
                           + bias_ref[2 * d + 1:2 * d + 2, :])
        log_a = neg_c_sp[d:d + 1, :] * r
        a = jnp.exp(log_a)
        gx = jnp.sqrt(jnp.tanh(-log_a) * (1.0 + a * a)) * (i * xc)
        return a, gx

    def fwd_body(c, h):
        t0 = pl.multiple_of(c * tc, tc)
        a, gx = gated(t0, waf_ref, wif_ref, 0)
        hs, h = _chunk_scan(a, gx, h, False)
        hf_ref[pl.ds(t0, tc), :] = hs
        return h

    h_f = lax.fori_loop(0, nchunk, fwd_body, h0_ref[0, 0:1, :])
    fin_ref[0, 0:1, :] = h_f

    def bwd_body(c, h):
        t0 = pl.multiple_of((nchunk - 1 - c) * tc, tc)
        a, gx = gated(t0, wab_ref, wib_ref, 1)
        hs, h = _chunk_scan(a, gx, h, True)
        ga = ga_ref[0, pl.ds(t0, tc), :].astype(F32)
        y = (hf_ref[pl.ds(t0, tc), :] + hs) * jax.nn.gelu(ga)
        y_ref[0, pl.ds(t0, tc), :] = y.astype(y_ref.dtype)
        return h

    h_b = lax.fori_loop(0, nchunk, bwd_body, h0_ref[0, 1:2, :])
    fin_ref[0, 1:2, :] = h_b


def _lru(proj, p, h0):
    bsz, length, _ = proj.shape
    cw_ = GROUP_W
    nsplit = BRANCH_W // cw_
    blk = lambda off: pl.BlockSpec((1, length, cw_), lambda b, j: (b, 0, off // cw_ + j))
    vec = lambda rows: pl.BlockSpec((rows, cw_), lambda b, j: (0, j))
    wspec = pl.BlockSpec((cw_, cw_), lambda b, j: (j, j))
    st = pl.BlockSpec((1, 2, cw_), lambda b, j: (b, 0, j))
    return pl.pallas_call(
        functools.partial(_lru_kernel, length=length),
        grid=(bsz, nsplit),
        in_specs=[blk(COL_XA), blk(COL_GA), vec(4), vec(1), wspec, wspec, wspec, wspec,
                  vec(4), vec(2), st],
        out_specs=[pl.BlockSpec((1, length, cw_), lambda b, j: (b, 0, j)), st],
        out_shape=[jax.ShapeDtypeStruct((bsz, length, BRANCH_W), BF16),
                   jax.ShapeDtypeStruct((bsz, 2, BRANCH_W), F32)],
        scratch_shapes=[pltpu.VMEM((length + 2 * PAD, cw_), F32),
                        pltpu.VMEM((length, cw_), F32)],
        compiler_params=_cp(("parallel", "parallel"), VMEM_BIG),
        name="rglru",
    )(proj, proj, p["lru_conv_w"], p["lru_conv_b"], p["lru_waf"], p["lru_wif"],
      p["lru_wab"], p["lru_wib"], p["lru_bias"], p["lru_lambda"], h0)


def _ssd_kernel(xs_ref, bc_ref, z_ref, dt_ref, cwx_ref, cbx_ref, cwb_ref, cbb_ref,
                dtb_ref, alog_ref, dch_ref, tri_ref, ex_ref, h0_ref, y_ref, fin_ref,
                padx_ref, padb_ref, xc_ref, yd_ref, st_ref, m_ref, xdt_ref, csx_ref, bmt_ref,
                *, length):
    q = SSM_CHUNK
    nchunk = length // q
    gw = xs_ref.shape[-1]
    nh = HEADS_PER_GROUP
    hp = SSM_HEADDIM
    ns = SSM_STATE
    _fill_padded(padx_ref, xs_ref, length, gw, q)
    _fill_padded(padb_ref, bc_ref, length, 2 * ns, q)
    cwx, cbx, cwb, cbb = cwx_ref[...], cbx_ref[...], cwb_ref[...], cbb_ref[...]

    def conv_body(c, carry):
        t0 = pl.multiple_of(c * q, q)
        xc_ref[pl.ds(t0, q), 0:gw] = jax.nn.silu(_conv_rows(padx_ref, t0, q, cwx, cbx, 2)).astype(BF16)
        xc_ref[pl.ds(t0, q), gw:gw + 2 * ns] = jax.nn.silu(
            _conv_rows(padb_ref, t0, q, cwb, cbb, 2)).astype(BF16)
        return carry

    lax.fori_loop(0, nchunk, conv_body, 0)

    dtb = dtb_ref[...]
    a_neg = -jnp.exp(alog_ref[...])
    rid = lax.broadcasted_iota(jnp.int32, (q, q), 0)
    cid = lax.broadcasted_iota(jnp.int32, (q, q), 1)
    lower = rid >= cid
    upper = rid <= cid
    lane_head = lax.broadcasted_iota(jnp.int32, (q, gw), 1) // hp

    def prepare(t0, d, slot):
        keep = lower if d == 0 else upper
        dt = _softplus(dt_ref[0, pl.ds(t0, q), :] + dtb)
        cs = jnp.dot(tri_ref[d], jnp.concatenate(_split3(dt * a_neg), axis=0),
                     preferred_element_type=F32)
        cs_t = cs.T
        dt_x = jnp.dot(jnp.concatenate(_split3(dt)[:2], axis=1), ex_ref[d, :, 0:gw],
                       preferred_element_type=F32)
        cs_all = jnp.dot(jnp.concatenate(_split3(cs)[:2], axis=1), ex_ref[d],
                         preferred_element_type=F32)
        cs_x = cs_all[:, 0:gw]
        xs = xc_ref[pl.ds(t0, q), 0:gw].astype(F32)
        bm = xc_ref[pl.ds(t0, q), gw:gw + ns]
        cm = xc_ref[pl.ds(t0, q), gw + ns:gw + 2 * ns]
        bm_t = bm.astype(F32).T.astype(BF16)
        cb = jnp.dot(cm, bm_t, preferred_element_type=F32)
        xdt_ref[slot, d] = xs * dt_x
        csx_ref[slot, d] = cs_x
        bmt_ref[slot, d] = bm_t
        for j in range(nh):
            lane = d * nh + j
            seg = cs_all[:, gw + j * q:gw + (j + 1) * q] - cs_t[lane:lane + 1, :]
            m_ref[slot, d, j * q:(j + 1) * q, :] = (
                cb * jnp.exp(jnp.where(keep, seg, -jnp.inf))).astype(BF16)

    def consume(t0, d, slot):
        edge = q - 1 if d == 0 else 0
        xdt = xdt_ref[slot, d]
        cs_x = csx_ref[slot, d]
        cm = xc_ref[pl.ds(t0, q), gw + ns:gw + 2 * ns]
        y_all = jnp.dot(m_ref[slot, d], xdt.astype(BF16), preferred_element_type=F32)
        y_diag = y_all[(nh - 1) * q:nh * q]
        for j in range(nh - 2, -1, -1):
            y_diag = jnp.where(lane_head == j, y_all[j * q:(j + 1) * q], y_diag)
        st = st_ref[d]
        y_off = jnp.dot(cm, st.astype(BF16), preferred_element_type=F32) * jnp.exp(cs_x)
        cs_edge = cs_x[edge:edge + 1, :]
        xw = (xdt * jnp.exp(cs_edge - cs_x)).astype(BF16)
        st_ref[d] = jnp.exp(cs_edge) * st + jnp.dot(bmt_ref[slot, d], xw,
                                                    preferred_element_type=F32)
        yd_ref[d, pl.ds(t0, q), :] = y_diag + y_off

    st_ref[...] = h0_ref[0, :, 0]

    def times(c):
        return pl.multiple_of(c * q, q), pl.multiple_of((nchunk - 1 - c) * q, q)

    def prepare_pair(c, slot):
        tf, tb = times(c)
        prepare(tf, 0, slot)
        prepare(tb, 1, slot)

    def sweep_body(c, carry):
        slot = c % 2
        prepare_pair(jnp.minimum(c + 1, nchunk - 1), 1 - slot)
        tf, tb = times(c)
        consume(tf, 0, slot)
        consume(tb, 1, slot)
        return carry

    prepare_pair(0, 0)
    lax.fori_loop(0, nchunk, sweep_body, 0, unroll=2 if nchunk % 2 == 0 else 1)
    fin_ref[0, :, 0] = st_ref[...]
    dch = dch_ref[...]

    def combine_body(c, carry):
        t0 = pl.multiple_of(c * q, q)
        xs = xc_ref[pl.ds(t0, q), 0:gw].astype(F32)
        z = z_ref[0, pl.ds(t0, q), :].astype(F32)
        ys = yd_ref[0, pl.ds(t0, q), :] + yd_ref[1, pl.ds(t0, q), :] + dch * xs
        y_ref[0, pl.ds(t0, q), :] = (ys * jax.nn.silu(z)).astype(y_ref.dtype)
        return carry

    lax.fori_loop(0, nchunk, combine_body, 0)


def _split3(x):
    hi = x.astype(BF16)
    rest = x - hi.astype(F32)
    mid = rest.astype(BF16)
    return hi, mid, (rest - mid.astype(F32)).astype(BF16)


def _ssd_tables():
    q, nh, hp, gw = SSM_CHUNK, HEADS_PER_GROUP, SSM_HEADDIM, GROUP_W
    r = jnp.arange(q)
    lower = (r[:, None] >= r[None, :]).astype(BF16)
    tri = jnp.stack([jnp.tile(lower, (1, 3)), jnp.tile(lower.T, (1, 3))])
    src = jnp.arange(LANES)[:, None]
    col = jnp.arange(gw + nh * q)[None, :]
    head_of_col = jnp.where(col < gw, col // hp, (col - gw) // q)
    expand = jnp.stack([jnp.tile((src == d * nh + head_of_col).astype(BF16), (2, 1))
                        for d in range(2)])
    return tri, expand


def _ssd(proj, dt, p, h0):
    bsz, length, _ = proj.shape
    gw, ns, nh, hp = GROUP_W, SSM_STATE, HEADS_PER_GROUP, SSM_HEADDIM
    tri, expand = _ssd_tables()
    blk = lambda off: pl.BlockSpec((1, length, gw), lambda b, g: (b, 0, off // gw + g))
    vec = lambda rows, w: pl.BlockSpec((rows, w), lambda b, g: (0, g))
    full = lambda a: pl.BlockSpec(a.shape, lambda b, g: (0,) * a.ndim)
    st = pl.BlockSpec((1, 2, 1, ns, gw), lambda b, g: (b, 0, g, 0, 0))
    return pl.pallas_call(
        functools.partial(_ssd_kernel, length=length),
        grid=(bsz, SSM_GROUPS),
        in_specs=[blk(COL_XS), blk(COL_BC), blk(COL_Z),
                  pl.BlockSpec((1, length, LANES), lambda b, g: (b, 0, g)),
                  vec(4, gw), vec(1, gw), vec(4, 2 * ns), vec(1, 2 * ns),
                  vec(1, LANES), vec(1, LANES), vec(1, gw), full(tri), full(expand), st],
        out_specs=[pl.BlockSpec((1, length, gw), lambda b, g: (b, 0, g)), st],
        out_shape=[jax.ShapeDtypeStruct((bsz, length, BRANCH_W), BF16),
                   jax.ShapeDtypeStruct((bsz, 2, SSM_GROUPS, ns, gw), F32)],
        scratch_shapes=[pltpu.VMEM((length + 2 * PAD, gw), F32),
                        pltpu.VMEM((length + 2 * PAD, 2 * ns), F32),
                        pltpu.VMEM((length, gw + 2 * ns), BF16),
                        pltpu.VMEM((2, length, gw), F32),
                        pltpu.VMEM((2, ns, gw), F32),
                        pltpu.VMEM((2, 2, nh * SSM_CHUNK, SSM_CHUNK), BF16),
                        pltpu.VMEM((2, 2, SSM_CHUNK, gw), F32),
                        pltpu.VMEM((2, 2, SSM_CHUNK, gw), F32),
                        pltpu.VMEM((2, 2, ns, SSM_CHUNK), BF16)],
        compiler_params=_cp(("parallel", "parallel"), VMEM_BIG),
        name="ssd",
    )(proj, proj, proj, dt, p["ssm_cw_x"], p["ssm_cb_x"], p["ssm_cw_bc"], p["ssm_cb_bc"],
      p["ssm_dtb"], p["ssm_alog"], p["ssm_dch"], tri, expand, h0)


HY_CONV_CHUNK = 256


def _hy_conv_kernel(u_ref, w_ref, b_ref, o_ref, pad_ref, *, length):
    width = u_ref.shape[-1]
    tc = min(HY_CONV_CHUNK, length)
    _fill_padded(pad_ref, u_ref, length, width, tc)
    w, b = w_ref[...], b_ref[...]

    def body(c, carry):
        t0 = pl.multiple_of(c * tc, tc)
        o_ref[0, 0, pl.ds(t0, tc), :] = _conv_rows(pad_ref, t0, tc, w, b, 1).astype(o_ref.dtype)
        return carry

    lax.fori_loop(0, length // tc, body, 0)


def _hy_conv(proj, p):
    bsz, length, _ = proj.shape
    w = BRANCH_W
    return pl.pallas_call(
        functools.partial(_hy_conv_kernel, length=length),
        grid=(bsz, 3),
        in_specs=[pl.BlockSpec((1, length, w), lambda b, j: (b, 0, COL_HY // w + j)),
                  pl.BlockSpec((3, w), lambda b, j: (0, j)),
                  pl.BlockSpec((1, w), lambda b, j: (0, j))],
        out_specs=pl.BlockSpec((1, 1, length, w), lambda b, j: (j, b, 0, 0)),
        out_shape=jax.ShapeDtypeStruct((3, bsz, length, w), BF16),
        scratch_shapes=[pltpu.VMEM((length + 2 * PAD, w), F32)],
        compiler_params=_cp(("parallel", "parallel"), VMEM_BIG),
        name="hyena_short_conv",
    )(proj, p["hy_conv_w"], p["hy_conv_b"])


def _hy_filter_kernel(f_ref, w1_ref, b1_ref, w2_ref, b2_ref, w3_ref, fr_ref, dl_ref, o_ref):
    feats = f_ref[...]
    h = jnp.sin(fr_ref[0:1, :] * (jnp.dot(feats, w1_ref[...], precision=HIGHEST,
                                          preferred_element_type=F32) + b1_ref[...]))
    h = jnp.sin(fr_ref[1:2, :] * (jnp.dot(h, w2_ref[...], precision=HIGHEST,
                                          preferred_element_type=F32) + b2_ref[...]))
    h = jnp.dot(h, w3_ref[...], precision=HIGHEST, preferred_element_type=F32)
    window = jnp.exp(-feats[:, 0:1] * dl_ref[...])
    fwd = feats[:, HY_EMB:HY_EMB + 1] * window
    bwd = feats[:, HY_EMB + 1:HY_EMB + 2] * window
    w = BRANCH_W
    for o in range(HY_ORDER):
        o_ref[o] = h[:, 2 * o * w:(2 * o + 1) * w] * fwd + h[:, (2 * o + 1) * w:(2 * o + 2) * w] * bwd


def _hy_taps(length, n, p):
    pos = jnp.arange(length, dtype=F32)[:, None]
    t01 = jnp.linspace(0.0, 1.0, length, dtype=F32)[:, None]
    bands = jnp.linspace(1e-4, HY_BANDS - 1, HY_BANDS, dtype=F32)
    ang = (2.0 * math.pi / length) * pos * bands
    feats = jnp.concatenate([t01, jnp.cos(ang), jnp.sin(ang)], axis=-1)

    def flagged(f, col):
        flags = jnp.zeros((f.shape[0], LANES - HY_EMB), F32).at[:, col].set(1.0)
        return jnp.concatenate([f, flags], axis=1)

    feats = jnp.concatenate([flagged(feats, 0), jnp.zeros((n - 2 * length + 1, LANES), F32),
                             flagged(feats[:0:-1], 1)], axis=0)
    max_decay = math.log(HY_TARGET) / HY_FAST_DECAY
    min_decay = math.log(HY_TARGET) / HY_SLOW_DECAY
    deltas = jnp.abs(jnp.linspace(min_decay, max_decay, BRANCH_W, dtype=F32))[None, :]
    tl = 512
    ncol = HY_ORDER * 2 * BRANCH_W
    full = lambda r, c: pl.BlockSpec((r, c), lambda i: (0, 0))
    return pl.pallas_call(
        _hy_filter_kernel,
        grid=(n // tl,),
        in_specs=[pl.BlockSpec((tl, LANES), lambda i: (i, 0)),
                  full(LANES, HY_FFN), full(1, HY_FFN), full(HY_FFN, HY_FFN), full(1, HY_FFN),
                  full(HY_FFN, ncol), full(2, HY_FFN), full(1, BRANCH_W)],
        out_specs=pl.BlockSpec((HY_ORDER, tl, BRANCH_W), lambda i: (0, i, 0)),
        out_shape=jax.ShapeDtypeStruct((HY_ORDER, n, BRANCH_W), F32),
        compiler_params=_cp(("parallel",)),
        name="hyena_filters",
    )(feats, p["hy_w1"], p["hy_b1"], p["hy_w2"], p["hy_b2"], p["hy_w3"], p["hy_freq"], deltas)


def _dft_tables(n1):
    n2 = FFT_N2
    n = n1 * n2
    two_pi = 2.0 * math.pi

    def cs(idx, period):
        ang = (two_pi / period) * (idx % period).astype(F32)
        return jnp.cos(ang), jnp.sin(ang)

    k1 = jnp.arange(n1, dtype=jnp.int32)
    c, s = cs(k1[:, None] * k1[None, :], n1)
    half = n1 // 2
    ch, sh = c[:, :half], s[:, :half]
    m1_re = jnp.concatenate([ch, sh], axis=1)
    m1_im = jnp.concatenate([-sh, ch], axis=1)
    m1 = jnp.stack([m1_re, m1_im], axis=1).reshape(2 * n1, n1)
    m1_taps = jnp.stack([c, -s], axis=1).reshape(2 * n1, n1)
    ci, si = ch.T / n, sh.T / n
    m2_re = jnp.stack([ci, -si], axis=2).reshape(half, 2 * n1)
    m2_im = jnp.stack([si, ci], axis=2).reshape(half, 2 * n1)
    m2 = jnp.concatenate([m2_re, m2_im], axis=0)
    k2 = jnp.arange(n2, dtype=jnp.int32)
    idx = k2[None, None, :] * (k2[None, :, None] * n1 + k1[:, None, None])
    gc, gs = cs(idx, n)
    g_fwd = jnp.concatenate([jnp.concatenate([gc, gs], axis=2),
                             jnp.concatenate([-gs, gc], axis=2)], axis=1)
    gct, gst = jnp.swapaxes(gc, 1, 2), jnp.swapaxes(gs, 1, 2)
    g_inv = jnp.concatenate([jnp.concatenate([gct, -gst], axis=2),
                             jnp.concatenate([gst, gct], axis=2)], axis=1)
    return m1, m1_taps, m2, g_fwd, g_inv


def _stage1_kernel(m_ref, *refs, nin, prec):
    o_ref = refs[-1]
    parts = [r[0] for r in refs[:nin]]
    x = parts[0] if nin == 1 else jnp.concatenate(parts, axis=0)
    o_ref[0] = jnp.dot(m_ref[...], x, precision=prec,
                       preferred_element_type=F32).astype(o_ref.dtype)


def _fft_chunk(ncols):
    return min(8192, ncols)


FFT_SUB = 16
FFT_NSUB = FFT_N2 // FFT_SUB


def _stage1_data_kernel(m_ref, x_ref, o_ref):
    rows = x_ref.shape[0] * x_ref.shape[1] * FFT_SUB
    x = x_ref[...].reshape(rows, x_ref.shape[-1])
    t = jnp.dot(m_ref[...], x, preferred_element_type=F32)
    o_ref[0, 0] = t.astype(o_ref.dtype).reshape(o_ref.shape[2:])


def _stage1_data(zsrc, zi, mbig1):
    nsrc, bsz, lp, w = zsrc.shape
    half = lp // FFT_N2
    npair = bsz // 2
    zv = zsrc.reshape(nsrc, bsz, half, FFT_NSUB, FFT_SUB, w)
    return pl.pallas_call(
        _stage1_data_kernel,
        grid=(npair, FFT_NSUB),
        in_specs=[pl.BlockSpec(mbig1.shape, lambda p, j: (0, 0)),
                  pl.BlockSpec((None, 2, half, None, FFT_SUB, w),
                               lambda p, j: (zi, p, 0, j, 0, 0))],
        out_specs=pl.BlockSpec((1, 1, 4 * half, FFT_SUB, w), lambda p, j: (p, j, 0, 0, 0)),
        out_shape=jax.ShapeDtypeStruct((npair, FFT_NSUB, 4 * half, FFT_SUB, w), BF16),
        compiler_params=_cp(("parallel", "parallel"), VMEM_MID),
        name="dft_stage1",
    )(mbig1, zv)


def _stage1_taps(taps, m1t):
    nord, n1, ncols = taps.shape
    ch = _fft_chunk(ncols)
    return pl.pallas_call(
        functools.partial(_stage1_kernel, nin=1, prec=HIGHEST),
        grid=(nord, ncols // ch),
        in_specs=[pl.BlockSpec(m1t.shape, lambda p, j: (0, 0)),
                  pl.BlockSpec((1, n1, ch), lambda p, j: (p, 0, j))],
        out_specs=pl.BlockSpec((1, 2 * n1, ch), lambda p, j: (p, 0, j)),
        out_shape=jax.ShapeDtypeStruct((nord, 2 * n1, ncols), F32),
        compiler_params=_cp(("parallel", "parallel")),
        name="dft_stage1_taps",
    )(m1t, taps)


def _mid_taps_kernel(g_ref, t_ref, o_ref):
    o_ref[0, 0] = jnp.dot(g_ref[0], t_ref[0, 0], precision=HIGHEST, preferred_element_type=F32)


def _mid_taps(t, g_fwd):
    nord, n1, rows, w = t.shape
    return pl.pallas_call(
        _mid_taps_kernel,
        grid=(n1, nord),
        in_specs=[pl.BlockSpec((1, rows, rows), lambda k, o: (k, 0, 0)),
                  pl.BlockSpec((1, 1, rows, w), lambda k, o: (o, k, 0, 0))],
        out_specs=pl.BlockSpec((1, 1, rows, w), lambda k, o: (o, k, 0, 0)),
        out_shape=jax.ShapeDtypeStruct(t.shape, F32),
        compiler_params=_cp(("parallel", "parallel")),
        name="dft_mid_taps",
    )(g_fwd, t)


def _mid_kernel(gf_ref, gi_ref, k_ref, t_ref, o_ref):
    n2 = FFT_N2
    w = t_ref.shape[-1]
    kr, ki = k_ref[0, :n2], k_ref[0, n2:]
    for p in range(t_ref.shape[0]):
        t = jnp.concatenate([t_ref[p, :, 0, 0].reshape(n2, w),
                             t_ref[p, :, 0, 1].reshape(n2, w)], axis=0)
        z = jnp.dot(gf_ref[0], t, preferred_element_type=F32)
        zr, zi = z[:n2], z[n2:]
        y = jnp.concatenate([zr * kr - zi * ki, zr * ki + zi * kr], axis=0).astype(BF16)
        u = jnp.dot(gi_ref[0], y, preferred_element_type=F32).astype(o_ref.dtype)
        o_ref[p, :, 0, 0] = u[:n2].reshape(FFT_NSUB, FFT_SUB, w)
        o_ref[p, :, 0, 1] = u[n2:].reshape(FFT_NSUB, FFT_SUB, w)


def _mid(t, kspec, g_fwd, g_inv):
    npair, nsub, rows, sub, w = t.shape
    n1 = rows // 2
    tv = t.reshape(npair, nsub, n1, 2, sub, w)
    gspec = pl.BlockSpec((1, 2 * FFT_N2, 2 * FFT_N2), lambda k: (k, 0, 0))
    tspec = pl.BlockSpec((npair, nsub, 1, 2, sub, w), lambda k: (0, 0, k, 0, 0, 0))
    out = pl.pallas_call(
        _mid_kernel,
        grid=(n1,),
        in_specs=[gspec, gspec,
                  pl.BlockSpec((1, 2 * FFT_N2, w), lambda k: (k, 0, 0)),
                  tspec],
        out_specs=tspec,
        out_shape=jax.ShapeDtypeStruct(tv.shape, BF16),
        compiler_params=_cp(("parallel",)),
        name="dft_mid",
    )(g_fwd, g_inv, kspec, tv)
    return out.reshape(t.shape)


def _last_kernel(m_ref, u_ref, z_ref, x_ref, bias_ref, o_ref):
    w = u_ref.shape[-1]
    u = u_ref[0, 0].reshape(u_ref.shape[2] * FFT_SUB, w)
    y = jnp.dot(m_ref[...], u, preferred_element_type=F32)
    rows = y.shape[0]
    z = z_ref[...].reshape(rows, w)
    x = x_ref[...].reshape(rows, w)
    conv = (y + z.astype(F32) * bias_ref[...]).astype(BF16)
    o_ref[...] = (x * conv).astype(o_ref.dtype).reshape(o_ref.shape)


def _last(u, zsrc, zi, xsrc, xi, bias_row, mbig2):
    npair, nsub, rows, sub, w = u.shape
    half = rows // 4
    bsz, lp = 2 * npair, half * FFT_N2
    view = lambda a: a.reshape(a.shape[0], bsz, half, nsub, sub, w)
    tok = lambda k: pl.BlockSpec((None, 2, half, None, sub, w), lambda p, j: (k, p, 0, j, 0, 0))
    out = pl.pallas_call(
        _last_kernel,
        grid=(npair, nsub),
        in_specs=[pl.BlockSpec(mbig2.shape, lambda p, j: (0, 0)),
                  pl.BlockSpec((1, 1, rows, sub, w), lambda p, j: (p, j, 0, 0, 0)),
                  tok(zi), tok(xi),
                  pl.BlockSpec((1, w), lambda p, j: (0, 0))],
        out_specs=tok(0),
        out_shape=jax.ShapeDtypeStruct((1, bsz, half, nsub, sub, w), BF16),
        compiler_params=_cp(("parallel", "parallel"), VMEM_MID),
        name="dft_last",
    )(mbig2, u, view(zsrc), view(xsrc), bias_row)
    return out.reshape(1, bsz, lp, w)


def _hyena_spectra(length, p):
    lp = max(length, FFT_MIN_LEN)
    n1 = 2 * lp // FFT_N2
    n = 2 * lp
    w = BRANCH_W
    taps = _hy_taps(length, n, p)
    m1, m1t, m2, g_fwd, g_inv = _dft_tables(n1)
    t = _stage1_taps(taps.reshape(HY_ORDER, n1, FFT_N2 * w), m1t)
    kspec = _mid_taps(t.reshape(HY_ORDER, n1, 2 * FFT_N2, w), g_fwd)
    eye = jnp.eye(FFT_SUB, dtype=F32)
    return kspec, (jnp.kron(m1, eye).astype(BF16), jnp.kron(m2, eye).astype(BF16),
                   g_fwd.astype(BF16), g_inv.astype(BF16))


def _hyena(proj, p, kspec, tables):
    bsz, length, _ = proj.shape
    mbig1, mbig2, g_fwd, g_inv = tables
    lp = mbig2.shape[0] // (2 * FFT_SUB) * FFT_N2
    uc = _hy_conv(proj, p)
    if lp != length:
        uc = jnp.pad(uc, ((0, 0), (0, 0), (0, lp - length), (0, 0)))
    zsrc, zi = uc, 0
    for order in range(HY_ORDER):
        t = _stage1_data(zsrc, zi, mbig1)
        u = _mid(t, kspec[order], g_fwd, g_inv)
        zsrc, zi = _last(u, zsrc, zi, uc, order + 1, p["hy_bias"][order:order + 1], mbig2), 0
    return zsrc[0, :, :length]


def _merge_kernel(x_ref, ya_ref, yb_ref, yc_ref, g0_ref, g1_ref, g2_ref, wb_ref, wo_ref,
                  sn_ref, n1_ref, gate_ref, o_ref):
    yc = _rms(yc_ref[0].astype(F32), sn_ref[...]).astype(BF16)
    merged = None
    for k, (y, g_ref) in enumerate(((ya_ref[0], g0_ref), (yb_ref[0], g1_ref), (yc, g2_ref))):
        term = jax.nn.sigmoid(g_ref[0].astype(F32)) * jnp.dot(y, wb_ref[k],
                                                              preferred_element_type=F32)
        merged = term if merged is None else merged + term
    out = jnp.dot(merged.astype(BF16), wo_ref[...], preferred_element_type=F32)
    o_ref[0] = x_ref[0] + gate_ref[0] * _rms(out, n1_ref[...])


def _merge(x, proj, ya, yb, yc, p, norm1, mod, row_of, k_gate):
    bsz, length, _ = x.shape
    tm = min(512, length)
    w = BRANCH_W
    tok = lambda c: pl.BlockSpec((1, tm, c), lambda b, i: (b, i, 0))
    gspec = lambda k: pl.BlockSpec((1, tm, D_MODEL),
                                   lambda b, i: (b, i, COL_GATES // D_MODEL + k))
    return pl.pallas_call(
        _merge_kernel,
        grid=(bsz, length // tm),
        in_specs=[tok(D_MODEL), tok(w), tok(w), tok(w), gspec(0), gspec(1), gspec(2),
                  pl.BlockSpec((N_BRANCH, w, D_MODEL), lambda b, i: (0, 0, 0)),
                  pl.BlockSpec((D_MODEL, D_MODEL), lambda b, i: (0, 0)),
                  pl.BlockSpec((1, w), lambda b, i: (0, 0)),
                  pl.BlockSpec((1, D_MODEL), lambda b, i: (0, 0)),
                  pl.BlockSpec((1, 1, D_MODEL), lambda b, i: (row_of(b), 0, k_gate))],
        out_specs=tok(D_MODEL),
        out_shape=jax.ShapeDtypeStruct(x.shape, F32),
        compiler_params=_cp(("parallel", "parallel"), VMEM_BIG),
        name="merge_out",
    )(x, ya, yb, yc, proj, proj, proj, p["w_branch"], p["w_out"], p["ssm_norm"],
      norm1.reshape(1, D_MODEL), mod)


FFN_CH = 256


def _ffn_kernel(x_ref, xp_ref, xn_ref, n2_ref, sh_ref, sc_ref, wu_ref, cw_ref, cb_ref, wd_ref,
                n3_ref, gate_ref, o_ref, hx_ref, h_ref, *, grid_w, tm):
    i = pl.program_id(1)
    nt = pl.num_programs(1)
    rows = tm + 2 * grid_w
    g2, sc, sh = n2_ref[...], sc_ref[0], sh_ref[0]

    def modulated(x):
        return _rms(x, g2) * (1.0 + sc) + sh

    hx_ref[0:grid_w] = (modulated(xp_ref[0]) * (i > 0).astype(F32)).astype(BF16)
    hx_ref[grid_w:grid_w + tm] = modulated(x_ref[0]).astype(BF16)
    hx_ref[grid_w + tm:rows] = (modulated(xn_ref[0]) * (i < nt - 1).astype(F32)).astype(BF16)

    col = lax.broadcasted_iota(jnp.int32, (rows, FFN_CH), 0) % grid_w
    not_first = col != 0
    not_last = col != grid_w - 1

    def conv(c0):
        ext = jnp.dot(hx_ref[...], wu_ref[:, c0:c0 + FFN_CH], preferred_element_type=F32)
        left = jnp.where(not_first, pltpu.roll(ext, 1, 0), 0.0).astype(BF16)
        right = jnp.where(not_last, pltpu.roll(ext, rows - 1, 0), 0.0).astype(BF16)
        mid = ext.astype(BF16)
        acc = cb_ref[:, c0:c0 + FFN_CH].astype(BF16)
        for dr in range(3):
            base = dr * grid_w
            for dc, src in enumerate((left, mid, right)):
                k = dr * 3 + dc
                acc = acc + cw_ref[k:k + 1, c0:c0 + FFN_CH].astype(BF16) * src[base:base + tm]
        return acc

    for j in range(FFN_HIDDEN // FFN_CH):
        val = conv(j * FFN_CH)
        gate = conv(FFN_HIDDEN + j * FFN_CH)
        h_ref[:, j * FFN_CH:(j + 1) * FFN_CH] = jax.nn.gelu(gate) * val
    down = jnp.dot(h_ref[...], wd_ref[...], preferred_element_type=F32)
    o_ref[0] = x_ref[0] + gate_ref[0] * _rms(down, n3_ref[...])


def _ffn(x, p, norm2, norm3, mod, row_of, k_shift, grid_w):
    bsz, length, _ = x.shape
    tm = min(512, length)
    ncol = 2 * FFN_HIDDEN
    per = tm // grid_w
    nblk = length // grid_w
    const = lambda shape: pl.BlockSpec(shape, lambda b, i: (0,) * len(shape),
                                       pipeline_mode=pl.Buffered(1))
    modv = lambda k: pl.BlockSpec((1, 1, D_MODEL), lambda b, i: (row_of(b), 0, k))
    return pl.pallas_call(
        functools.partial(_ffn_kernel, grid_w=grid_w, tm=tm),
        grid=(bsz, length // tm),
        in_specs=[pl.BlockSpec((1, tm, D_MODEL), lambda b, i: (b, i, 0)),
                  pl.BlockSpec((1, grid_w, D_MODEL),
                               lambda b, i: (b, jnp.maximum(i * per - 1, 0), 0)),
                  pl.BlockSpec((1, grid_w, D_MODEL),
                               lambda b, i: (b, jnp.minimum((i + 1) * per, nblk - 1), 0)),
                  const((1, D_MODEL)), modv(k_shift), modv(k_shift + 1),
                  const((D_MODEL, ncol)), const((9, ncol)), const((1, ncol)),
                  const((FFN_HIDDEN, D_MODEL)), const((1, D_MODEL)), modv(k_shift + 2)],
        out_specs=pl.BlockSpec((1, tm, D_MODEL), lambda b, i: (b, i, 0)),
        out_shape=jax.ShapeDtypeStruct(x.shape, F32),
        scratch_shapes=[pltpu.VMEM((tm + 2 * grid_w, D_MODEL), BF16),
                        pltpu.VMEM((tm, FFN_HIDDEN), BF16)],
        compiler_params=_cp(("parallel", "parallel"), VMEM_BIG),
        name="ffn",
    )(x, x, x, norm2.reshape(1, D_MODEL), mod, mod, p["ffn_w_up"], p["ffn_conv_w"],
      p["ffn_conv_b"], p["ffn_w_down"], norm3.reshape(1, D_MODEL), mod)


def _block_diag(w):
    nb, k, j = w.shape
    eye = jnp.eye(nb, dtype=w.dtype)
    return (eye[:, None, :, None] * w[:, :, None, :]).reshape(nb * k, nb * j)


def _layer_params(l, a):
    ns = SSM_STATE
    w_in = a["w_in"][l]
    xbc0 = BRANCH_W
    b0 = xbc0 + BRANCH_W
    c0 = b0 + SSM_BC
    bc_cols = jnp.concatenate([w_in[:, b0:b0 + ns], w_in[:, c0:c0 + ns],
                               w_in[:, b0 + ns:b0 + 2 * ns], w_in[:, c0 + ns:c0 + 2 * ns]], axis=1)
    out0 = STATE_COLS
    w_main = jnp.concatenate([
        w_in[:, :BRANCH_W],
        w_in[:, xbc0:xbc0 + BRANCH_W],
        bc_cols,
        w_in[:, out0:out0 + BRANCH_W],
        w_in[:, out0 + BRANCH_W:out0 + BRANCH_W + HY_PROJ],
        w_in[:, out0 + BRANCH_W + HY_PROJ:out0 + 2 * BRANCH_W + HY_PROJ],
        w_in[:, out0 + 2 * BRANCH_W + HY_PROJ:],
    ], axis=1).astype(BF16)
    dt0 = BRANCH_W + SSM_XBC

    def per_group(v2):
        rows = []
        for g in range(SSM_GROUPS):
            sel = v2[:, g * HEADS_PER_GROUP:(g + 1) * HEADS_PER_GROUP].reshape(-1)
            rows.append(jnp.pad(sel, (0, LANES - sel.shape[0])))
        return jnp.concatenate(rows)[None, :]

    w_dt_cols = w_in[:, dt0:dt0 + 2 * SSM_HEADS].reshape(D_MODEL, 2, SSM_HEADS)
    w_dt = jnp.concatenate([
        jnp.pad(w_dt_cols[:, :, g * HEADS_PER_GROUP:(g + 1) * HEADS_PER_GROUP].reshape(D_MODEL, -1),
                ((0, 0), (0, LANES - 2 * HEADS_PER_GROUP))) for g in range(SSM_GROUPS)],
        axis=1).astype(BF16)
    scw = a["ssm_conv_w"][l]
    scb = a["ssm_conv_b"][l][None, :]
    perm_bc = lambda v: jnp.concatenate(
        [v[:, BRANCH_W:BRANCH_W + ns], v[:, BRANCH_W + SSM_BC:BRANCH_W + SSM_BC + ns],
         v[:, BRANCH_W + ns:BRANCH_W + 2 * ns], v[:, BRANCH_W + SSM_BC + ns:]], axis=1)
    return {
        "w_main": w_main, "w_dt": w_dt,
        "lru_conv_w": a["lru_conv_w"][l], "lru_conv_b": a["lru_conv_b"][l][None, :],
        "lru_waf": _block_diag(a["lru_w_a"][l, 0]).astype(BF16),
        "lru_wif": _block_diag(a["lru_w_i"][l, 0]).astype(BF16),
        "lru_wab": _block_diag(a["lru_w_a"][l, 1]).astype(BF16),
        "lru_wib": _block_diag(a["lru_w_i"][l, 1]).astype(BF16),
        "lru_bias": jnp.stack([a["lru_b_a"][l, 0], a["lru_b_i"][l, 0],
                               a["lru_b_a"][l, 1], a["lru_b_i"][l, 1]]),
        "lru_lambda": a["lru_lambda"][l],
        "hy_conv_w": a["hy_conv_w"][l], "hy_conv_b": a["hy_conv_b"][l][None, :],
        "hy_w1": jnp.pad(a["hy_w1"][l], ((0, LANES - HY_EMB), (0, 0))),
        "hy_b1": a["hy_b1"][l][None, :], "hy_w2": a["hy_w2"][l], "hy_b2": a["hy_b2"][l][None, :],
        "hy_w3": a["hy_w3"][l], "hy_freq": a["hy_freq"][l], "hy_bias": a["hy_bias"][l],
        "ssm_cw_x": scw[:, :BRANCH_W], "ssm_cb_x": scb[:, :BRANCH_W],
        "ssm_cw_bc": perm_bc(scw), "ssm_cb_bc": perm_bc(scb),
        "ssm_dtb": per_group(a["ssm_dt_bias"][l]), "ssm_alog": per_group(a["ssm_a_log"][l]),
        "ssm_dch": jnp.repeat(a["ssm_d"][l], SSM_HEADDIM)[None, :],
        "ssm_norm": a["ssm_norm"][l][None, :],
        "w_branch": a["w_branch"][l].astype(BF16), "w_out": a["w_out"][l].astype(BF16),
        "ffn_w_up": a["ffn_w_up"][l].astype(BF16),
        "ffn_conv_w": a["ffn_conv_w"][l].reshape(9, 2 * FFN_HIDDEN),
        "ffn_conv_b": a["ffn_conv_b"][l][None, :],
        "ffn_w_down": a["ffn_w_down"][l].astype(BF16),
    }


def _mixer_states(tokens, p, norm0, mod, row_of, base, init):
    proj, dt = _norm_mm(tokens, norm0, mod, row_of, base, base + 1, p["w_main"], p["w_dt"],
                        tn=MAIN_COLS // 2)
    ya, lru_fin = _lru(proj, p, init[0])
    yc, ssm_fin = _ssd(proj, dt, p, init[1])
    return proj, ya, yc, (lru_fin, ssm_fin)


def _layer_tokens(tokens, p, norms_l, mod, row_of, init, hy, grid_w):
    proj, ya, yc, finals = _mixer_states(tokens, p, norms_l[0], mod, row_of, 0, init)
    yb = _hyena(proj, p, *hy)
    tokens = _merge(tokens, proj, ya, yb, yc, p, norms_l[1], mod, row_of, 2)
    tokens = _ffn(tokens, p, norms_l[2], norms_l[3], mod, row_of, 3, grid_w)
    return tokens, finals


def kernel(x, c, ctx, c_ctx, mod_w, mod_b, norms, w_in, lru_conv_w, lru_conv_b, lru_w_a, lru_b_a,
           lru_w_i, lru_b_i, lru_lambda, hy_conv_w, hy_conv_b, hy_w1, hy_b1, hy_w2, hy_b2, hy_w3,
           hy_freq, hy_bias, ssm_conv_w, ssm_conv_b, ssm_dt_bias, ssm_a_log, ssm_d, ssm_norm,
           w_branch, w_out, ffn_w_up, ffn_conv_w, ffn_conv_b, ffn_w_down):
    a = dict(w_in=w_in, lru_conv_w=lru_conv_w, lru_conv_b=lru_conv_b, lru_w_a=lru_w_a,
             lru_b_a=lru_b_a, lru_w_i=lru_w_i, lru_b_i=lru_b_i, lru_lambda=lru_lambda,
             hy_conv_w=hy_conv_w, hy_conv_b=hy_conv_b, hy_w1=hy_w1, hy_b1=hy_b1, hy_w2=hy_w2,
             hy_b2=hy_b2, hy_w3=hy_w3, hy_freq=hy_freq, hy_bias=hy_bias, ssm_conv_w=ssm_conv_w,
             ssm_conv_b=ssm_conv_b, ssm_dt_bias=ssm_dt_bias, ssm_a_log=ssm_a_log, ssm_d=ssm_d,
             ssm_norm=ssm_norm, w_branch=w_branch, w_out=w_out, ffn_w_up=ffn_w_up,
             ffn_conv_w=ffn_conv_w, ffn_conv_b=ffn_conv_b, ffn_w_down=ffn_w_down)
    bsz, length, _ = x.shape
    ctx_len = ctx.shape[1]
    depth = mod_w.shape[0]
    rows = -(-(bsz + 1) // 8) * 8
    cc = jnp.concatenate([c, c_ctx[None, :], jnp.zeros((rows - bsz - 1, D_MODEL), F32)], axis=0)
    mod_all = _modulation(cc, mod_w, mod_b)
    lat_row = lambda b: b
    ctx_row = lambda b: bsz
    zero_init = (jnp.zeros((bsz, 2, BRANCH_W), F32),
                 jnp.zeros((bsz, 2, SSM_GROUPS, SSM_STATE, GROUP_W), F32))
    for l in range(depth):
        last = l == depth - 1
        p = _layer_params(l, a)
        mod = mod_all[l].reshape(rows, 1, 6 * D_MODEL)
        if last:
            _, _, _, ctx_states = _mixer_states(ctx, p, norms[l, 0], mod, ctx_row, 0, zero_init)
        else:
            hy_ctx = _hyena_spectra(ctx_len, p)
            ctx, ctx_states = _layer_tokens(ctx, p, norms[l], mod, ctx_row, zero_init, hy_ctx,
                                            ctx_len)
        hy_lat = _hyena_spectra(length, p)
        x, _ = _layer_tokens(x, p, norms[l], mod, lat_row, ctx_states, hy_lat, GRID_W)
    return x
```
